```python
import math
import jax, jax.numpy as jnp
from jax import lax
import numpy as np

D_MODEL = 1024
BATCH = 4
SEQ = 8192
DEPTH = 1

MIX_WIDTH = D_MODEL
POOL_WIDTH = MIX_WIDTH // 2
POOL_WINDOWS = (2, 4, 8, 16)
N_POOL_GROUPS = len(POOL_WINDOWS)
POOL_GROUP_DIM = POOL_WIDTH // N_POOL_GROUPS
ATTN_WIDTH = MIX_WIDTH - POOL_WIDTH
HEAD_DIM = 64
N_HEADS = ATTN_WIDTH // HEAD_DIM
DILATED_PATTERNS = ((128, 1), (512, 4), (2048, 16))
BLOCK = 128
N_BUCKETS = 32
MAX_DISTANCE = 2048
D_FF = 4 * D_MODEL
NORM_EPS = 1e-6
NEG_INF = -1e30

kernel_name = "hybrid_pool_dilated_attn_layer"


def rms_norm(x, g):
    xf = x.astype(jnp.float32)
    y = xf * lax.rsqrt(jnp.mean(xf * xf, axis=-1, keepdims=True) + NORM_EPS)
    return (y * g.astype(jnp.float32)).astype(x.dtype)


def t5_bucket(dist):
    max_exact = N_BUCKETS // 2
    d_f = jnp.maximum(dist, 1).astype(jnp.float32)
    large = max_exact + (jnp.log(d_f / max_exact) / math.log(MAX_DISTANCE / max_exact)
                         * (N_BUCKETS - max_exact)).astype(jnp.int32)
    large = jnp.minimum(large, N_BUCKETS - 1)
    return jnp.where(dist < max_exact, dist, large)


def causal_mean_minus_self(u, window):
    uf = u.astype(jnp.float32)
    s = uf.shape[1]
    c = jnp.pad(jnp.cumsum(uf, axis=1), ((0, 0), (window, 0), (0, 0)))
    window_sum = c[:, window:] - c[:, :s]
    count = jnp.minimum(jnp.arange(1, s + 1), window).astype(jnp.float32)
    return window_sum / count[None, :, None] - uf


def multiscale_pool_mixer(u, pool_w, pool_scale):
    b, s, _ = u.shape
    ug = u.reshape(b, s, N_POOL_GROUPS, POOL_GROUP_DIM)
    pooled = jnp.stack([causal_mean_minus_self(ug[:, :, g], w)
                        for g, w in enumerate(POOL_WINDOWS)], axis=2)
    mixed = jnp.einsum('bsgc,gcd->bsgd', pooled.astype(u.dtype), pool_w)
    mixed = mixed * pool_scale.reshape(N_POOL_GROUPS, POOL_GROUP_DIM)
    return mixed.reshape(b, s, POOL_WIDTH).astype(u.dtype)


def dilated_window_attention(q, k, v, rel_bias, window, dilation):
    b, h, s, d = q.shape
    w = window // dilation
    assert w <= BLOCK
    L = s // dilation
    nb = -(-L // BLOCK)
    Lp = nb * BLOCK

    def to_blocks(t):
        t = t.reshape(b, h, L, dilation, d).transpose(0, 1, 3, 2, 4)
        t = jnp.pad(t, ((0, 0), (0, 0), (0, 0), (0, Lp - L), (0, 0)))
        return t.reshape(b, h, dilation, nb, BLOCK, d)

    def with_prev(t):
        prev = jnp.pad(t[:, :, :, :-1], ((0, 0), (0, 0), (0, 0), (1, 0), (0, 0), (0, 0)))
        return jnp.concatenate([prev, t], axis=-2)

    qb = to_blocks(q)
    kb = with_prev(to_blocks(k))
    vb = with_prev(to_blocks(v))

    qq = jnp.arange(BLOCK)[:, None]
    kk = jnp.arange(2 * BLOCK)[None, :]
    dist = qq + BLOCK - kk
    rel_ok = (dist >= 0) & (dist <= w)
    mask = rel_ok[None] & ((jnp.arange(nb)[:, None, None] > 0) | (kk >= BLOCK)[None])
    bucket = t5_bucket(jnp.clip(dist, 0, w) * dilation)
    bias = rel_bias[bucket].astype(jnp.float32).transpose(2, 0, 1)

    scores = jnp.einsum('bhrnqd,bhrnkd->bhrnqk', qb, kb,
                        preferred_element_type=jnp.float32)
    scores = jnp.where(mask[None, None, None], scores + bias[None, :, None, None], NEG_INF)
    m = jnp.max(scores, axis=-1, keepdims=True)
    p = jnp.exp(scores - m)
    l = jnp.sum(p, axis=-1, keepdims=True)
    acc = jnp.einsum('bhrnqk,bhrnkd->bhrnqd', p.astype(v.dtype), vb,
                     preferred_element_type=jnp.float32)

    def from_blocks(t):
        e = t.shape[-1]
        t = t.reshape(b, h, dilation, Lp, e)[:, :, :, :L]
        return t.transpose(0, 1, 3, 2, 4).reshape(b, h, s, e)

    return from_blocks(acc), from_blocks(m), from_blocks(l)


def dilated_attention_mixer(q, k, v, q_norm_g, k_norm_g, rel_bias):
    b, s, _ = q.shape
    split = lambda t: t.reshape(b, s, N_HEADS, HEAD_DIM).transpose(0, 2, 1, 3)
    qh = rms_norm(split(q), q_norm_g) * (HEAD_DIM ** -0.5)
    kh = rms_norm(split(k), k_norm_g)
    vh = split(v)
    outs = [dilated_window_attention(qh, kh, vh, rel_bias, w, dl)
            for (w, dl) in DILATED_PATTERNS]
    m_all = jnp.max(jnp.stack([o[1] for o in outs]), axis=0)
    num = sum(jnp.exp(mi - m_all) * acc for acc, mi, _ in outs)
    den = sum(jnp.exp(mi - m_all) * li for _, mi, li in outs)
    o = (num / den).astype(q.dtype)
    return o.transpose(0, 2, 1, 3).reshape(b, s, ATTN_WIDTH)


def setup_inputs(seed: int = 0) -> dict:
    key = jax.random.key(seed)
    ks = jax.random.split(key, 12)
    f32 = jnp.float32
    nrm = lambda k, shape, scale: jax.random.normal(k, shape, f32) * scale
    return {
        "x": nrm(ks[0], (BATCH, SEQ, D_MODEL), 1.0),
        "mix_norm_g": 1.0 + nrm(ks[1], (D_MODEL,), 0.02),
        "w_in": nrm(ks[2], (D_MODEL, POOL_WIDTH + 3 * ATTN_WIDTH), D_MODEL ** -0.5),
        "pool_w": nrm(ks[3], (N_POOL_GROUPS, POOL_GROUP_DIM, POOL_GROUP_DIM), POOL_GROUP_DIM ** -0.5),
        "pool_scale": 1.0 + nrm(ks[4], (POOL_WIDTH,), 0.02),
        "q_norm_g": 1.0 + nrm(ks[5], (HEAD_DIM,), 0.02),
        "k_norm_g": 1.0 + nrm(ks[6], (HEAD_DIM,), 0.02),
        "rel_bias": nrm(ks[7], (N_BUCKETS, N_HEADS), 0.1),
        "w_out": nrm(ks[8], (MIX_WIDTH, D_MODEL), MIX_WIDTH ** -0.5),
        "mlp_norm_g": 1.0 + nrm(ks[9], (D_MODEL,), 0.02),
        "w_up": nrm(ks[10], (D_MODEL, D_FF), D_MODEL ** -0.5),
        "w_down": nrm(ks[11], (D_FF, D_MODEL), D_FF ** -0.5),
    }


def reference(x, mix_norm_g, w_in, pool_w, pool_scale, q_norm_g, k_norm_g, rel_bias,
              w_out, mlp_norm_g, w_up, w_down):
    h = x
    for _ in range(DEPTH):
        a = rms_norm(h, mix_norm_g)
        proj = jnp.einsum('bsd,de->bse', a, w_in)
        u_pool = proj[..., :POOL_WIDTH]
        q = proj[..., POOL_WIDTH:POOL_WIDTH + ATTN_WIDTH]
        k = proj[..., POOL_WIDTH + ATTN_WIDTH:POOL_WIDTH + 2 * ATTN_WIDTH]
        v = proj[..., POOL_WIDTH + 2 * ATTN_WIDTH:]
        y_pool = multiscale_pool_mixer(u_pool, pool_w, pool_scale)
        y_attn = dilated_attention_mixer(q, k, v, q_norm_g, k_norm_g, rel_bias)
        mixed = jnp.concatenate([y_pool, y_attn], axis=-1)
        h = h + jnp.einsum('bse,ed->bsd', mixed, w_out)
        c = rms_norm(h, mlp_norm_g)
        ff = jnp.square(jax.nn.relu(jnp.einsum('bsd,df->bsf', c, w_up)))
        h = h + jnp.einsum('bsf,fd->bsd', ff, w_down)
    return h
```

```python
import functools
import math

import numpy as np
import jax
import jax.numpy as jnp
from jax import lax
from jax.experimental import pallas as pl
from jax.experimental.pallas import tpu as pltpu

D_MODEL = 1024
POOL_WIDTH = 512
POOL_WINDOWS = (2, 4, 8, 16)
POOL_GROUP_DIM = 128
ATTN_WIDTH = 512
HEAD_DIM = 64
N_HEADS = 8
DILATIONS = (1, 4, 16)
WINDOW_STEPS = 128
N_BUCKETS = 32
MAX_DISTANCE = 2048
D_FF = 4096
NORM_EPS = 1e-6
NEG_INF = -1e30

LANES = 128
N_PAIRS = ATTN_WIDTH // LANES
MAX_DIL = 16
SUPER = WINDOW_STEPS * MAX_DIL
HALO = 16
TM_PROJ = 512
TM_MLP = 512
FF_CHUNK = 1024
VMEM_LIMIT = 56 * 1024 * 1024

F32 = jnp.float32
BF16 = jnp.bfloat16


def _rms(x, g):
    return x * lax.rsqrt(jnp.mean(x * x, axis=-1, keepdims=True) + NORM_EPS) * g


def _proj_kernel(x_ref, g_ref, w_ref, gsum_ref, qg_ref, kg_ref, pw_ref, ps_ref,
                 ypool_ref, q_ref, k_ref, v_ref, ubuf, *, tiles_per_seq):
    i = pl.program_id(0)
    tm = x_ref.shape[0]
    a = _rms(x_ref[...], g_ref[...]).astype(BF16)

    def head_norm(t, gain):
        sq = t * t
        hi = sq.astype(BF16)
        lo = (sq - hi.astype(F32)).astype(BF16)
        ssq = (jnp.dot(hi, gsum_ref[...], preferred_element_type=F32)
               + jnp.dot(lo, gsum_ref[...], preferred_element_type=F32))
        return t * lax.rsqrt(ssq * (1.0 / HEAD_DIM) + NORM_EPS) * gain

    def put_pairs(ref, t):
        for hp in range(N_PAIRS):
            ref[hp] = t[:, hp * LANES:(hp + 1) * LANES].astype(BF16)

    q = jnp.dot(a, w_ref[:, POOL_WIDTH:POOL_WIDTH + ATTN_WIDTH], preferred_element_type=F32)
    put_pairs(q_ref, head_norm(q, qg_ref[...]) * (HEAD_DIM ** -0.5))
    k = jnp.dot(a, w_ref[:, POOL_WIDTH + ATTN_WIDTH:POOL_WIDTH + 2 * ATTN_WIDTH],
                preferred_element_type=F32)
    put_pairs(k_ref, head_norm(k, kg_ref[...]))
    v = jnp.dot(a, w_ref[:, POOL_WIDTH + 2 * ATTN_WIDTH:], preferred_element_type=F32)
    put_pairs(v_ref, v)

    seq_tile = i % tiles_per_seq

    @pl.when(seq_tile == 0)
    def _():
        ubuf[0:HALO, :] = jnp.zeros((HALO, POOL_WIDTH), F32)

    @pl.when(seq_tile != 0)
    def _():
        ubuf[0:HALO, :] = ubuf[tm:tm + HALO, :]

    u = jnp.dot(a, w_ref[:, :POOL_WIDTH], preferred_element_type=F32)
    ubuf[HALO:HALO + tm, :] = u
    pos = seq_tile * tm + lax.broadcasted_iota(jnp.int32, (tm, 1), 0)
    for g, w in enumerate(POOL_WINDOWS):
        cols = slice(g * POOL_GROUP_DIM, (g + 1) * POOL_GROUP_DIM)
        wsum = ubuf[HALO:HALO + tm, cols]
        for d in range(1, w):
            wsum = wsum + ubuf[HALO - d:HALO - d + tm, cols]
        inv_count = 1.0 / jnp.minimum(pos + 1, w).astype(F32)
        pooled = wsum * inv_count - ubuf[HALO:HALO + tm, cols]
        mixed = jnp.dot(pooled.astype(BF16), pw_ref[g], preferred_element_type=F32)
        ypool_ref[:, cols] = (mixed * ps_ref[:, cols]).astype(BF16)


def _proj_call(x2, mix_g, w_in, gsum, qg, kg, pool_w, pool_scale, seq):
    n = x2.shape[0]
    tm = TM_PROJ
    const = lambda shape: pl.BlockSpec(shape, lambda i: (0,) * len(shape))
    pair_spec = pl.BlockSpec((N_PAIRS, tm, LANES), lambda i: (0, i, 0))
    pair_shape = jax.ShapeDtypeStruct((N_PAIRS, n, LANES), BF16)
    return pl.pallas_call(
        functools.partial(_proj_kernel, tiles_per_seq=seq // tm),
        grid=(n // tm,),
        in_specs=[
            pl.BlockSpec((tm, D_MODEL), lambda i: (i, 0)),
            const((1, D_MODEL)),
            const((D_MODEL, POOL_WIDTH + 3 * ATTN_WIDTH)),
            const((ATTN_WIDTH, ATTN_WIDTH)),
            const((1, ATTN_WIDTH)),
            const((1, ATTN_WIDTH)),
            const((len(POOL_WINDOWS), POOL_GROUP_DIM, POOL_GROUP_DIM)),
            const((1, POOL_WIDTH)),
        ],
        out_specs=[pl.BlockSpec((tm, POOL_WIDTH), lambda i: (i, 0)), pair_spec, pair_spec, pair_spec],
        out_shape=[jax.ShapeDtypeStruct((n, POOL_WIDTH), BF16), pair_shape, pair_shape, pair_shape],
        scratch_shapes=[pltpu.VMEM((HALO + tm, POOL_WIDTH), F32)],
        compiler_params=pltpu.CompilerParams(dimension_semantics=("arbitrary",),
                                             vmem_limit_bytes=VMEM_LIMIT),
        name="proj",
    )(x2, mix_g, w_in, gsum, qg, kg, pool_w, pool_scale)


def _block_orders():
    out = []
    for dil in DILATIONS:
        ns = MAX_DIL // dil
        c = WINDOW_STEPS // ns
        slab = np.arange(ns)[:, None]
        q_step = (np.arange(c)[None, :] * ns + slab + WINDOW_STEPS).reshape(-1)
        rows = np.arange(2 * c)[None, :]
        k_step = np.where(rows < c, rows * ns + slab, (rows - c) * ns + slab + WINDOW_STEPS).reshape(-1)
        out.append((dil, ns, c, q_step, k_step))
    return out


def _bucket_tables():
    max_exact = N_BUCKETS // 2
    buckets, prev_cols = [], []
    for dil, ns, c, q_step, k_step in _block_orders():
        dist = q_step[:, None] - k_step[None, :]
        ok = (dist >= 0) & (dist <= WINDOW_STEPS)
        tok = np.clip(dist, 0, WINDOW_STEPS) * dil
        d_f = np.maximum(tok, 1).astype(np.float32)
        large = max_exact + (np.log(d_f / np.float32(max_exact)) / np.float32(math.log(MAX_DISTANCE / max_exact))
                             * np.float32(N_BUCKETS - max_exact)).astype(np.int32)
        large = np.minimum(large, N_BUCKETS - 1)
        bucket = np.where(tok < max_exact, tok, large)
        buckets.append(np.where(ok, bucket, -1).astype(np.int32))
        prev_cols.append(np.broadcast_to((k_step < WINDOW_STEPS)[None, :], dist.shape).astype(np.int32))
    return np.stack(buckets), np.stack(prev_cols)


def _bias_kernel(rel_ref, bucket_ref, prev_ref, out_ref):
    bucket = bucket_ref[0]
    is_prev = prev_ref[0] != 0
    for h in range(N_HEADS):
        tab = jnp.full(bucket.shape, NEG_INF, F32)
        for b in range(N_BUCKETS):
            tab = jnp.where(bucket == b, rel_ref[b, h], tab)
        rows = slice((h % 2) * WINDOW_STEPS, (h % 2 + 1) * WINDOW_STEPS)
        out_ref[0, 0, h // 2, rows, :] = tab
        out_ref[0, 1, h // 2, rows, :] = jnp.where(is_prev, NEG_INF, tab)


def _bias_call(rel_bias):
    bucket, prev_cols = _bucket_tables()
    npat = len(DILATIONS)
    tab_spec = pl.BlockSpec((1, WINDOW_STEPS, 2 * WINDOW_STEPS), lambda p: (p, 0, 0))
    return pl.pallas_call(
        _bias_kernel,
        grid=(npat,),
        in_specs=[pl.BlockSpec(memory_space=pltpu.SMEM), tab_spec, tab_spec],
        out_specs=pl.BlockSpec((1, 2, N_PAIRS, 2 * WINDOW_STEPS, 2 * WINDOW_STEPS),
                               lambda p: (p, 0, 0, 0, 0)),
        out_shape=jax.ShapeDtypeStruct((npat, 2, N_PAIRS, 2 * WINDOW_STEPS, 2 * WINDOW_STEPS), F32),
        name="bias",
    )(rel_bias, jnp.asarray(bucket), jnp.asarray(prev_cols))


def _attn_kernel(q_ref, kc_ref, kp_ref, vc_ref, vp_ref, bias_ref, o_ref,
                 q32, k32, v32, acc_s, m_s, l_s, *, supers_per_seq):
    bm = pl.program_id(1)
    first = jnp.where(bm % supers_per_seq == 0, 1, 0)
    w = WINDOW_STEPS

    for r in range(MAX_DIL):
        cols = slice(r * LANES, (r + 1) * LANES)
        q32[r] = q_ref[:, cols].astype(F32)
        k32[r, 0:w, :] = kp_ref[:, cols].astype(F32)
        k32[r, w:2 * w, :] = kc_ref[:, cols].astype(F32)
        v32[r, 0:w, :] = vp_ref[:, cols].astype(F32)
        v32[r, w:2 * w, :] = vc_ref[:, cols].astype(F32)

    head0 = lax.broadcasted_iota(jnp.int32, (w, LANES), 1) < HEAD_DIM

    def attend(p, slab0, ns, c, n, bias):
        step = MAX_DIL // ns
        rows_q = pl.ds(pl.multiple_of(c * n, 8), c)
        rows_k = pl.ds(pl.multiple_of(w - c + c * n, 8), 2 * c)
        if ns == 1:
            slabs = pl.ds(slab0, 1)
        else:
            slabs = pl.ds(slab0, ns, stride=step)
        q = q32[slabs, rows_q, :].reshape(w, LANES)
        kk = k32[slabs, rows_k, :].reshape(2 * w, LANES).astype(BF16)
        vv = v32[slabs, rows_k, :].reshape(2 * w, LANES).astype(BF16)
        zero = jnp.zeros_like(q)
        qm = jnp.concatenate([jnp.where(head0, q, zero), jnp.where(head0, zero, q)], axis=0).astype(BF16)
        s = lax.dot_general(qm, kk, (((1,), (1,)), ((), ())), preferred_element_type=F32) + bias
        m = jnp.max(s, axis=-1, keepdims=True)
        e = jnp.exp(s - m)
        l = jnp.sum(e, axis=-1, keepdims=True)
        pv = jnp.dot(e.astype(BF16), vv, preferred_element_type=F32)
        acc = jnp.where(head0, pv[:w], pv[w:])
        m_b = jnp.where(head0, m[:w], m[w:])
        l_b = jnp.where(head0, l[:w], l[w:])
        acc_s[p, slabs, rows_q, :] = acc.reshape(ns, c, LANES)
        m_s[p, slabs, rows_q, :] = m_b.reshape(ns, c, LANES)
        l_s[p, slabs, rows_q, :] = l_b.reshape(ns, c, LANES)

    def body16(r, carry):
        attend(2, r, 1, w, 0, bias_ref[2, first])
        return carry
    lax.fori_loop(0, MAX_DIL, body16, 0)

    for a in range(4):
        def body4(n, carry, a=a):
            attend(1, a, 4, 32, n, bias_ref[1, jnp.where(n == 0, first, 0)])
            return carry
        lax.fori_loop(0, 4, body4, 0)

    def body1(n, carry):
        attend(0, 0, 16, 8, n, bias_ref[0, jnp.where(n == 0, first, 0)])
        return carry
    lax.fori_loop(0, 16, body1, 0)

    def combine(r, carry):
        m0, m1, m2 = m_s[0, r], m_s[1, r], m_s[2, r]
        m_all = jnp.maximum(jnp.maximum(m0, m1), m2)
        w0, w1, w2 = jnp.exp(m0 - m_all), jnp.exp(m1 - m_all), jnp.exp(m2 - m_all)
        num = w0 * acc_s[0, r] + w1 * acc_s[1, r] + w2 * acc_s[2, r]
        den = w0 * l_s[0, r] + w1 * l_s[1, r] + w2 * l_s[2, r]
        q32[r] = num / den
        return carry
    lax.fori_loop(0, MAX_DIL, combine, 0)
    for r in range(MAX_DIL):
        o_ref[:, r * LANES:(r + 1) * LANES] = q32[r].astype(BF16)


def _attn_call(q, k, v, bias, batch, seq):
    n = q.shape[1]
    rows = n // MAX_DIL
    width = MAX_DIL * LANES
    q2, k2, v2 = (t.reshape(N_PAIRS, rows, width) for t in (q, k, v))
    supers_per_seq = seq // SUPER
    w = WINDOW_STEPS
    cur = pl.BlockSpec((None, w, width), lambda hp, bm: (hp, bm, 0))
    prev = pl.BlockSpec((None, w, width), lambda hp, bm: (hp, jnp.maximum(bm - 1, 0), 0))
    npat = len(DILATIONS)
    slab = lambda rws: pltpu.VMEM((MAX_DIL, rws, LANES), F32)
    part = pltpu.VMEM((npat, MAX_DIL, w, LANES), F32)
    out = pl.pallas_call(
        functools.partial(_attn_kernel, supers_per_seq=supers_per_seq),
        grid=(N_PAIRS, n // SUPER),
        in_specs=[cur, cur, prev, cur, prev,
                  pl.BlockSpec((npat, 2, None, 2 * w, 2 * w), lambda hp, bm: (0, 0, hp, 0, 0))],
        out_specs=cur,
        out_shape=jax.ShapeDtypeStruct((N_PAIRS, rows, width), BF16),
        scratch_shapes=[slab(w), slab(2 * w), slab(2 * w), part, part, part],
        compiler_params=pltpu.CompilerParams(dimension_semantics=("arbitrary", "arbitrary"),
                                             vmem_limit_bytes=VMEM_LIMIT),
        name="attn",
    )(q2, k2, k2, v2, v2, bias)
    return out.reshape(N_PAIRS, n, LANES)


def _mlp_kernel(x_ref, ypool_ref, yattn_ref, wo_ref, g_ref, wu_ref, wd_ref, o_ref):
    mixed = jnp.concatenate([ypool_ref[...]] + [yattn_ref[hp] for hp in range(N_PAIRS)], axis=-1)
    h = x_ref[...] + jnp.dot(mixed, wo_ref[...], preferred_element_type=F32)
    c = _rms(h, g_ref[...]).astype(BF16)
    acc = h
    for f in range(0, D_FF, FF_CHUNK):
        up = jnp.dot(c, wu_ref[:, f:f + FF_CHUNK], preferred_element_type=F32)
        ff = jnp.square(jnp.maximum(up, 0.0)).astype(BF16)
        acc = acc + jnp.dot(ff, wd_ref[f:f + FF_CHUNK, :], preferred_element_type=F32)
    o_ref[...] = acc


def _mlp_call(x2, ypool, yattn, w_out, mlp_g, w_up, w_down):
    n = x2.shape[0]
    tm = TM_MLP
    const = lambda shape: pl.BlockSpec(shape, lambda i: (0,) * len(shape),
                                       pipeline_mode=pl.Buffered(1))
    return pl.pallas_call(
        _mlp_kernel,
        grid=(n // tm,),
        in_specs=[
            pl.BlockSpec((tm, D_MODEL), lambda i: (i, 0)),
            pl.BlockSpec((tm, POOL_WIDTH), lambda i: (i, 0)),
            pl.BlockSpec((N_PAIRS, tm, LANES), lambda i: (0, i, 0)),
            const((D_MODEL, D_MODEL)),
            const((1, D_MODEL)),
            const((D_MODEL, D_FF)),
            const((D_FF, D_MODEL)),
        ],
        out_specs=pl.BlockSpec((tm, D_MODEL), lambda i: (i, 0)),
        out_shape=jax.ShapeDtypeStruct((n, D_MODEL), F32),
        compiler_params=pltpu.CompilerParams(dimension_semantics=("arbitrary",),
                                             vmem_limit_bytes=VMEM_LIMIT),
        name="mlp",
    )(x2, ypool, yattn, w_out, mlp_g, w_up, w_down)


def kernel(x, mix_norm_g, w_in, pool_w, pool_scale, q_norm_g, k_norm_g, rel_bias,
           w_out, mlp_norm_g, w_up, w_down):
    batch, seq, d = x.shape
    assert d == D_MODEL and seq % SUPER == 0 and seq % TM_PROJ == 0
    x2 = x.reshape(batch * seq, d)
    head_of = np.arange(ATTN_WIDTH) // HEAD_DIM
    gsum = jnp.asarray(head_of[:, None] == head_of[None, :], BF16)
    tile_heads = lambda g: jnp.tile(g.astype(F32), N_HEADS).reshape(1, ATTN_WIDTH)

    ypool, q, k, v = _proj_call(
        x2, mix_norm_g.reshape(1, d), w_in.astype(BF16), gsum,
        tile_heads(q_norm_g), tile_heads(k_norm_g), pool_w.astype(BF16),
        pool_scale.reshape(1, POOL_WIDTH), seq)
    bias = _bias_call(rel_bias)
    yattn = _attn_call(q, k, v, bias, batch, seq)
    y = _mlp_call(x2, ypool, yattn, w_out.astype(BF16), mlp_norm_g.reshape(1, d),
                  w_up.astype(BF16), w_down.astype(BF16))
    return y.reshape(batch, seq, d)
```

```python
import functools
import math

import numpy as np
import jax
import jax.numpy as jnp
from jax import lax
from jax.experimental import pallas as pl
from jax.experimental.pallas import tpu as pltpu

D_MODEL = 1024
POOL_WIDTH = 512
POOL_WINDOWS = (2, 4, 8, 16)
POOL_GROUP_DIM = 128
ATTN_WIDTH = 512
HEAD_DIM = 64
N_HEADS = 8
DILATIONS = (1, 4, 16)
WINDOW_STEPS = 128
N_BUCKETS = 32
MAX_DISTANCE = 2048
D_FF = 4096
NORM_EPS = 1e-6
NEG_INF = -1e30

LANES = 128
N_PAIRS = ATTN_WIDTH // LANES
MAX_DIL = 16
SUPER = WINDOW_STEPS * MAX_DIL
HALO = 16
TM_PROJ = 512
TM_MLP = 512
FF_CHUNK = 1024
VMEM_LIMIT = 56 * 1024 * 1024

F32 = jnp.float32
BF16 = jnp.bfloat16


def _rms(x, g):
    return x * lax.rsqrt(jnp.mean(x * x, axis=-1, keepdims=True) + NORM_EPS) * g


def _proj_kernel(x_ref, g_ref, w_ref, gsum_ref, qg_ref, kg_ref, pw_ref, ps_ref,
                 ypool_ref, q_ref, k_ref, v_ref, ubuf, *, tiles_per_seq):
    i = pl.program_id(0)
    tm = x_ref.shape[0]
    a = _rms(x_ref[...], g_ref[...]).astype(BF16)

    def head_norm(t, gain):
        sq = t * t
        hi = sq.astype(BF16)
        lo = (sq - hi.astype(F32)).astype(BF16)
        ssq = (jnp.dot(hi, gsum_ref[...], preferred_element_type=F32)
               + jnp.dot(lo, gsum_ref[...], preferred_element_type=F32))
        return t * lax.rsqrt(ssq * (1.0 / HEAD_DIM) + NORM_EPS) * gain

    def put_pairs(ref, t):
        for hp in range(N_PAIRS):
            ref[hp] = t[:, hp * LANES:(hp + 1) * LANES].astype(BF16)

    q = jnp.dot(a, w_ref[:, POOL_WIDTH:POOL_WIDTH + ATTN_WIDTH], preferred_element_type=F32)
    put_pairs(q_ref, head_norm(q, qg_ref[...]) * (HEAD_DIM ** -0.5))
    k = jnp.dot(a, w_ref[:, POOL_WIDTH + ATTN_WIDTH:POOL_WIDTH + 2 * ATTN_WIDTH],
                preferred_element_type=F32)
    put_pairs(k_ref, head_norm(k, kg_ref[...]))
    v = jnp.dot(a, w_ref[:, POOL_WIDTH + 2 * ATTN_WIDTH:], preferred_element_type=F32)
    put_pairs(v_ref, v)

    seq_tile = i % tiles_per_seq

    @pl.when(seq_tile == 0)
    def _():
        ubuf[0:HALO, :] = jnp.zeros((HALO, POOL_WIDTH), F32)

    @pl.when(seq_tile != 0)
    def _():
        ubuf[0:HALO, :] = ubuf[tm:tm + HALO, :]

    u = jnp.dot(a, w_ref[:, :POOL_WIDTH], preferred_element_type=F32)
    ubuf[HALO:HALO + tm, :] = u
    pos = seq_tile * tm + lax.broadcasted_iota(jnp.int32, (tm, 1), 0)
    for g, w in enumerate(POOL_WINDOWS):
        cols = slice(g * POOL_GROUP_DIM, (g + 1) * POOL_GROUP_DIM)
        wsum = ubuf[HALO:HALO + tm, cols]
        for d in range(1, w):
            wsum = wsum + ubuf[HALO - d:HALO - d + tm, cols]
        inv_count = 1.0 / jnp.minimum(pos + 1, w).astype(F32)
        pooled = wsum * inv_count - ubuf[HALO:HALO + tm, cols]
        mixed = jnp.dot(pooled.astype(BF16), pw_ref[g], preferred_element_type=F32)
        ypool_ref[:, cols] = (mixed * ps_ref[:, cols]).astype(BF16)


def _proj_call(x2, mix_g, w_in, gsum, qg, kg, pool_w, pool_scale, seq):
    n = x2.shape[0]
    tm = TM_PROJ
    const = lambda shape: pl.BlockSpec(shape, lambda i: (0,) * len(shape))
    pair_spec = pl.BlockSpec((N_PAIRS, tm, LANES), lambda i: (0, i, 0))
    pair_shape = jax.ShapeDtypeStruct((N_PAIRS, n, LANES), BF16)
    return pl.pallas_call(
        functools.partial(_proj_kernel, tiles_per_seq=seq // tm),
        grid=(n // tm,),
        in_specs=[
            pl.BlockSpec((tm, D_MODEL), lambda i: (i, 0)),
            const((1, D_MODEL)),
            const((D_MODEL, POOL_WIDTH + 3 * ATTN_WIDTH)),
            const((ATTN_WIDTH, ATTN_WIDTH)),
            const((1, ATTN_WIDTH)),
            const((1, ATTN_WIDTH)),
            const((len(POOL_WINDOWS), POOL_GROUP_DIM, POOL_GROUP_DIM)),
            const((1, POOL_WIDTH)),
        ],
        out_specs=[pl.BlockSpec((tm, POOL_WIDTH), lambda i: (i, 0)), pair_spec, pair_spec, pair_spec],
        out_shape=[jax.ShapeDtypeStruct((n, POOL_WIDTH), BF16), pair_shape, pair_shape, pair_shape],
        scratch_shapes=[pltpu.VMEM((HALO + tm, POOL_WIDTH), F32)],
        compiler_params=pltpu.CompilerParams(dimension_semantics=("arbitrary",),
                                             vmem_limit_bytes=VMEM_LIMIT),
        name="proj",
    )(x2, mix_g, w_in, gsum, qg, kg, pool_w, pool_scale)


def _block_orders():
    out = []
    for dil in DILATIONS:
        ns = MAX_DIL // dil
        c = WINDOW_STEPS // ns
        slab = np.arange(ns)[:, None]
        q_step = (np.arange(c)[None, :] * ns + slab + WINDOW_STEPS).reshape(-1)
        rows = np.arange(2 * c)[None, :]
        k_step = np.where(rows < c, rows * ns + slab, (rows - c) * ns + slab + WINDOW_STEPS).reshape(-1)
        out.append((dil, ns, c, q_step, k_step))
    return out


def _bucket_tables():
    max_exact = N_BUCKETS // 2
    buckets, prev_cols = [], []
    for dil, ns, c, q_step, k_step in _block_orders():
        dist = q_step[:, None] - k_step[None, :]
        ok = (dist >= 0) & (dist <= WINDOW_STEPS)
        tok = np.clip(dist, 0, WINDOW_STEPS) * dil
        d_f = np.maximum(tok, 1).astype(np.float32)
        large = max_exact + (np.log(d_f / np.float32(max_exact)) / np.float32(math.log(MAX_DISTANCE / max_exact))
                             * np.float32(N_BUCKETS - max_exact)).astype(np.int32)
        large = np.minimum(large, N_BUCKETS - 1)
        bucket = np.where(tok < max_exact, tok, large)
        buckets.append(np.where(ok, bucket, -1).astype(np.int32))
        prev_cols.append(np.broadcast_to((k_step < WINDOW_STEPS)[None, :], dist.shape).astype(np.int32))
    return np.stack(buckets), np.stack(prev_cols)


def _bias_kernel(rel_ref, bucket_ref, prev_ref, out_ref):
    bucket = bucket_ref[0]
    is_prev = prev_ref[0] != 0
    for h in range(N_HEADS):
        tab = jnp.full(bucket.shape, NEG_INF, F32)
        for b in range(N_BUCKETS):
            tab = jnp.where(bucket == b, rel_ref[b, h], tab)
        rows = slice((h % 2) * WINDOW_STEPS, (h % 2 + 1) * WINDOW_STEPS)
        out_ref[0, 0, h // 2, rows, :] = tab
        out_ref[0, 1, h // 2, rows, :] = jnp.where(is_prev, NEG_INF, tab)


def _bias_call(rel_bias):
    bucket, prev_cols = _bucket_tables()
    npat = len(DILATIONS)
    tab_spec = pl.BlockSpec((1, WINDOW_STEPS, 2 * WINDOW_STEPS), lambda p: (p, 0, 0))
    return pl.pallas_call(
        _bias_kernel,
        grid=(npat,),
        in_specs=[pl.BlockSpec(memory_space=pltpu.SMEM), tab_spec, tab_spec],
        out_specs=pl.BlockSpec((1, 2, N_PAIRS, 2 * WINDOW_STEPS, 2 * WINDOW_STEPS),
                               lambda p: (p, 0, 0, 0, 0)),
        out_shape=jax.ShapeDtypeStruct((npat, 2, N_PAIRS, 2 * WINDOW_STEPS, 2 * WINDOW_STEPS), F32),
        name="bias",
    )(rel_bias, jnp.asarray(bucket), jnp.asarray(prev_cols))


def _attn_kernel(q_ref, kc_ref, kp_ref, vc_ref, vp_ref, bias_ref, o_ref,
                 q32, k32, v32, acc_s, m_s, l_s, *, supers_per_seq):
    bm = pl.program_id(1)
    first = jnp.where(bm % supers_per_seq == 0, 1, 0)
    w = WINDOW_STEPS

    for r in range(MAX_DIL):
        cols = slice(r * LANES, (r + 1) * LANES)
        q32[r] = q_ref[:, cols].astype(F32)
        k32[r, 0:w, :] = kp_ref[:, cols].astype(F32)
        k32[r, w:2 * w, :] = kc_ref[:, cols].astype(F32)
        v32[r, 0:w, :] = vp_ref[:, cols].astype(F32)
        v32[r, w:2 * w, :] = vc_ref[:, cols].astype(F32)

    head0 = lax.broadcasted_iota(jnp.int32, (w, LANES), 1) < HEAD_DIM

    def attend(p, slab0, ns, c, n, bias):
        step = MAX_DIL // ns
        rows_q = pl.ds(pl.multiple_of(c * n, 8), c)
        rows_k = pl.ds(pl.multiple_of(w - c + c * n, 8), 2 * c)
        if ns == 1:
            slabs = pl.ds(slab0, 1)
        else:
            slabs = pl.ds(slab0, ns, stride=step)
        q = q32[slabs, rows_q, :].reshape(w, LANES)
        kk = k32[slabs, rows_k, :].reshape(2 * w, LANES).astype(BF16)
        vv = v32[slabs, rows_k, :].reshape(2 * w, LANES).astype(BF16)
        zero = jnp.zeros_like(q)
        qm = jnp.concatenate([jnp.where(head0, q, zero), jnp.where(head0, zero, q)], axis=0).astype(BF16)
        s = lax.dot_general(qm, kk, (((1,), (1,)), ((), ())), preferred_element_type=F32) + bias
        m = jnp.max(s, axis=-1, keepdims=True)
        e = jnp.exp(s - m)
        l = jnp.sum(e, axis=-1, keepdims=True)
        pv = jnp.dot(e.astype(BF16), vv, preferred_element_type=F32)
        acc = jnp.where(head0, pv[:w], pv[w:])
        m_b = jnp.where(head0, m[:w], m[w:])
        l_b = jnp.where(head0, l[:w], l[w:])
        acc_s[p, slabs, rows_q, :] = acc.reshape(ns, c, LANES)
        m_s[p, slabs, rows_q, :] = m_b.reshape(ns, c, LANES)
        l_s[p, slabs, rows_q, :] = l_b.reshape(ns, c, LANES)

    def body16(r, carry):
        attend(2, r, 1, w, 0, bias_ref[2, first])
        return carry
    lax.fori_loop(0, MAX_DIL, body16, 0, unroll=True)

    for a in range(4):
        def body4(n, carry, a=a):
            attend(1, a, 4, 32, n, bias_ref[1, jnp.where(n == 0, first, 0)])
            return carry
        lax.fori_loop(0, 4, body4, 0, unroll=True)

    def body1(n, carry):
        attend(0, 0, 16, 8, n, bias_ref[0, jnp.where(n == 0, first, 0)])
        return carry
    lax.fori_loop(0, 16, body1, 0, unroll=True)

    def combine(r, carry):
        m0, m1, m2 = m_s[0, r], m_s[1, r], m_s[2, r]
        m_all = jnp.maximum(jnp.maximum(m0, m1), m2)
        w0, w1, w2 = jnp.exp(m0 - m_all), jnp.exp(m1 - m_all), jnp.exp(m2 - m_all)
        num = w0 * acc_s[0, r] + w1 * acc_s[1, r] + w2 * acc_s[2, r]
        den = w0 * l_s[0, r] + w1 * l_s[1, r] + w2 * l_s[2, r]
        q32[r] = num / den
        return carry
    lax.fori_loop(0, MAX_DIL, combine, 0)
    for r in range(MAX_DIL):
        o_ref[:, r * LANES:(r + 1) * LANES] = q32[r].astype(BF16)


def _attn_call(q, k, v, bias, batch, seq):
    n = q.shape[1]
    rows = n // MAX_DIL
    width = MAX_DIL * LANES
    q2, k2, v2 = (t.reshape(N_PAIRS, rows, width) for t in (q, k, v))
    supers_per_seq = seq // SUPER
    w = WINDOW_STEPS
    cur = pl.BlockSpec((None, w, width), lambda hp, bm: (hp, bm, 0))
    prev = pl.BlockSpec((None, w, width), lambda hp, bm: (hp, jnp.maximum(bm - 1, 0), 0))
    npat = len(DILATIONS)
    slab = lambda rws: pltpu.VMEM((MAX_DIL, rws, LANES), F32)
    part = pltpu.VMEM((npat, MAX_DIL, w, LANES), F32)
    out = pl.pallas_call(
        functools.partial(_attn_kernel, supers_per_seq=supers_per_seq),
        grid=(N_PAIRS, n // SUPER),
        in_specs=[cur, cur, prev, cur, prev,
                  pl.BlockSpec((npat, 2, None, 2 * w, 2 * w), lambda hp, bm: (0, 0, hp, 0, 0))],
        out_specs=cur,
        out_shape=jax.ShapeDtypeStruct((N_PAIRS, rows, width), BF16),
        scratch_shapes=[slab(w), slab(2 * w), slab(2 * w), part, part, part],
        compiler_params=pltpu.CompilerParams(dimension_semantics=("arbitrary", "arbitrary"),
                                             vmem_limit_bytes=VMEM_LIMIT),
        name="attn",
    )(q2, k2, k2, v2, v2, bias)
    return out.reshape(N_PAIRS, n, LANES)


def _mlp_kernel(x_ref, ypool_ref, yattn_ref, wo_ref, g_ref, wu_ref, wd_ref, o_ref):
    mixed = jnp.concatenate([ypool_ref[...]] + [yattn_ref[hp] for hp in range(N_PAIRS)], axis=-1)
    h = x_ref[...] + jnp.dot(mixed, wo_ref[...], preferred_element_type=F32)
    c = _rms(h, g_ref[...]).astype(BF16)
    acc = h
    for f in range(0, D_FF, FF_CHUNK):
        up = jnp.dot(c, wu_ref[:, f:f + FF_CHUNK], preferred_element_type=F32)
        ff = jnp.square(jnp.maximum(up, 0.0)).astype(BF16)
        acc = acc + jnp.dot(ff, wd_ref[f:f + FF_CHUNK, :], preferred_element_type=F32)
    o_ref[...] = acc


def _mlp_call(x2, ypool, yattn, w_out, mlp_g, w_up, w_down):
    n = x2.shape[0]
    tm = TM_MLP
    const = lambda shape: pl.BlockSpec(shape, lambda i: (0,) * len(shape),
                                       pipeline_mode=pl.Buffered(1))
    return pl.pallas_call(
        _mlp_kernel,
        grid=(n // tm,),
        in_specs=[
            pl.BlockSpec((tm, D_MODEL), lambda i: (i, 0)),
            pl.BlockSpec((tm, POOL_WIDTH), lambda i: (i, 0)),
            pl.BlockSpec((N_PAIRS, tm, LANES), lambda i: (0, i, 0)),
            const((D_MODEL, D_MODEL)),
            const((1, D_MODEL)),
            const((D_MODEL, D_FF)),
            const((D_FF, D_MODEL)),
        ],
        out_specs=pl.BlockSpec((tm, D_MODEL), lambda i: (i, 0)),
        out_shape=jax.ShapeDtypeStruct((n, D_MODEL), F32),
        compiler_params=pltpu.CompilerParams(dimension_semantics=("arbitrary",),
                                             vmem_limit_bytes=VMEM_LIMIT),
        name="mlp",
    )(x2, ypool, yattn, w_out, mlp_g, w_up, w_down)


def kernel(x, mix_norm_g, w_in, pool_w, pool_scale, q_norm_g, k_norm_g, rel_bias,
           w_out, mlp_norm_g, w_up, w_down):
    batch, seq, d = x.shape
    assert d == D_MODEL and seq % SUPER == 0 and seq % TM_PROJ == 0
    x2 = x.reshape(batch * seq, d)
    head_of = np.arange(ATTN_WIDTH) // HEAD_DIM
    gsum = jnp.asarray(head_of[:, None] == head_of[None, :], BF16)
    tile_heads = lambda g: jnp.tile(g.astype(F32), N_HEADS).reshape(1, ATTN_WIDTH)

    ypool, q, k, v = _proj_call(
        x2, mix_norm_g.reshape(1, d), w_in.astype(BF16), gsum,
        tile_heads(q_norm_g), tile_heads(k_norm_g), pool_w.astype(BF16),
        pool_scale.reshape(1, POOL_WIDTH), seq)
    bias = _bias_call(rel_bias)
    yattn = _attn_call(q, k, v, bias, batch, seq)
    y = _mlp_call(x2, ypool, yattn, w_out.astype(BF16), mlp_norm_g.reshape(1, d),
                  w_up.astype(BF16), w_down.astype(BF16))
    return y.reshape(batch, seq, d)
```

```python
import functools
import math

import numpy as np
import jax
import jax.numpy as jnp
from jax import lax
from jax.experimental import pallas as pl
from jax.experimental.pallas import tpu as pltpu

D_MODEL = 1024
POOL_WIDTH = 512
POOL_WINDOWS = (2, 4, 8, 16)
POOL_GROUP_DIM = 128
ATTN_WIDTH = 512
HEAD_DIM = 64
N_HEADS = 8
DILATIONS = (1, 4, 16)
WINDOW_STEPS = 128
N_BUCKETS = 32
MAX_DISTANCE = 2048
D_FF = 4096
NORM_EPS = 1e-6
NEG_INF = -1e30

LANES = 128
N_PAIRS = ATTN_WIDTH // LANES
MXU_DIM = 256
MAX_DIL = 16
SUPER = WINDOW_STEPS * MAX_DIL
HALO = 16
TM_PROJ = 512
TM_MLP = 512
FF_CHUNK = 1024
VMEM_LIMIT = 56 * 1024 * 1024

F32 = jnp.float32
BF16 = jnp.bfloat16


def _rms(x, g):
    return x * lax.rsqrt(jnp.mean(x * x, axis=-1, keepdims=True) + NORM_EPS) * g


def _proj_kernel(x_ref, g_ref, w_ref, gsum_ref, qg_ref, kg_ref, pw_ref, ps_ref,
                 ypool_ref, q_ref, k_ref, v_ref, ubuf, *, tiles_per_seq):
    i = pl.program_id(0)
    tm = x_ref.shape[0]
    a = _rms(x_ref[...], g_ref[...]).astype(BF16)

    def head_norm(t, gain):
        sq = t * t
        hi = sq.astype(BF16)
        lo = (sq - hi.astype(F32)).astype(BF16)
        gw = gsum_ref.shape[0]
        ssq = jnp.concatenate(
            [jnp.dot(hi[:, c:c + gw], gsum_ref[...], preferred_element_type=F32)
             + jnp.dot(lo[:, c:c + gw], gsum_ref[...], preferred_element_type=F32)
             for c in range(0, ATTN_WIDTH, gw)], axis=-1)
        return t * lax.rsqrt(ssq * (1.0 / HEAD_DIM) + NORM_EPS) * gain

    def put_pairs(ref, t):
        for hp in range(N_PAIRS):
            ref[hp] = t[:, hp * LANES:(hp + 1) * LANES].astype(BF16)

    q = jnp.dot(a, w_ref[:, POOL_WIDTH:POOL_WIDTH + ATTN_WIDTH], preferred_element_type=F32)
    put_pairs(q_ref, head_norm(q, qg_ref[...]) * (HEAD_DIM ** -0.5))
    k = jnp.dot(a, w_ref[:, POOL_WIDTH + ATTN_WIDTH:POOL_WIDTH + 2 * ATTN_WIDTH],
                preferred_element_type=F32)
    put_pairs(k_ref, head_norm(k, kg_ref[...]))
    v = jnp.dot(a, w_ref[:, POOL_WIDTH + 2 * ATTN_WIDTH:], preferred_element_type=F32)
    put_pairs(v_ref, v)

    seq_tile = i % tiles_per_seq

    @pl.when(seq_tile == 0)
    def _():
        ubuf[0:HALO, :] = jnp.zeros((HALO, POOL_WIDTH), F32)

    @pl.when(seq_tile != 0)
    def _():
        ubuf[0:HALO, :] = ubuf[tm:tm + HALO, :]

    u = jnp.dot(a, w_ref[:, :POOL_WIDTH], preferred_element_type=F32)
    ubuf[HALO:HALO + tm, :] = u
    pos = seq_tile * tm + lax.broadcasted_iota(jnp.int32, (tm, 1), 0)
    for g, w in enumerate(POOL_WINDOWS):
        cols = slice(g * POOL_GROUP_DIM, (g + 1) * POOL_GROUP_DIM)
        wsum = ubuf[HALO:HALO + tm, cols]
        for d in range(1, w):
            wsum = wsum + ubuf[HALO - d:HALO - d + tm, cols]
        inv_count = 1.0 / jnp.minimum(pos + 1, w).astype(F32)
        pooled = wsum * inv_count - ubuf[HALO:HALO + tm, cols]
        mixed = jnp.dot(pooled.astype(BF16), pw_ref[g], preferred_element_type=F32)
        ypool_ref[:, cols] = (mixed * ps_ref[:, cols]).astype(BF16)


def _proj_call(x2, mix_g, w_in, gsum, qg, kg, pool_w, pool_scale, seq):
    n = x2.shape[0]
    tm = TM_PROJ
    const = lambda shape: pl.BlockSpec(shape, lambda i: (0,) * len(shape))
    pair_spec = pl.BlockSpec((N_PAIRS, tm, LANES), lambda i: (0, i, 0))
    pair_shape = jax.ShapeDtypeStruct((N_PAIRS, n, LANES), BF16)
    return pl.pallas_call(
        functools.partial(_proj_kernel, tiles_per_seq=seq // tm),
        grid=(n // tm,),
        in_specs=[
            pl.BlockSpec((tm, D_MODEL), lambda i: (i, 0)),
            const((1, D_MODEL)),
            const((D_MODEL, POOL_WIDTH + 3 * ATTN_WIDTH)),
            const((MXU_DIM, MXU_DIM)),
            const((1, ATTN_WIDTH)),
            const((1, ATTN_WIDTH)),
            const((len(POOL_WINDOWS), POOL_GROUP_DIM, POOL_GROUP_DIM)),
            const((1, POOL_WIDTH)),
        ],
        out_specs=[pl.BlockSpec((tm, POOL_WIDTH), lambda i: (i, 0)), pair_spec, pair_spec, pair_spec],
        out_shape=[jax.ShapeDtypeStruct((n, POOL_WIDTH), BF16), pair_shape, pair_shape, pair_shape],
        scratch_shapes=[pltpu.VMEM((HALO + tm, POOL_WIDTH), F32)],
        compiler_params=pltpu.CompilerParams(dimension_semantics=("arbitrary",),
                                             vmem_limit_bytes=VMEM_LIMIT),
        name="proj",
    )(x2, mix_g, w_in, gsum, qg, kg, pool_w, pool_scale)


def _block_orders():
    out = []
    for dil in DILATIONS:
        ns = MAX_DIL // dil
        c = WINDOW_STEPS // ns
        slab = np.arange(ns)[:, None]
        q_step = (np.arange(c)[None, :] * ns + slab + WINDOW_STEPS).reshape(-1)
        rows = np.arange(2 * c)[None, :]
        k_step = np.where(rows < c, rows * ns + slab, (rows - c) * ns + slab + WINDOW_STEPS).reshape(-1)
        out.append((dil, ns, c, q_step, k_step))
    return out


def _bucket_tables():
    max_exact = N_BUCKETS // 2
    buckets, prev_cols = [], []
    for dil, ns, c, q_step, k_step in _block_orders():
        dist = q_step[:, None] - k_step[None, :]
        ok = (dist >= 0) & (dist <= WINDOW_STEPS)
        tok = np.clip(dist, 0, WINDOW_STEPS) * dil
        d_f = np.maximum(tok, 1).astype(np.float32)
        large = max_exact + (np.log(d_f / np.float32(max_exact)) / np.float32(math.log(MAX_DISTANCE / max_exact))
                             * np.float32(N_BUCKETS - max_exact)).astype(np.int32)
        large = np.minimum(large, N_BUCKETS - 1)
        bucket = np.where(tok < max_exact, tok, large)
        buckets.append(np.where(ok, bucket, -1).astype(np.int32))
        prev_cols.append(np.broadcast_to((k_step < WINDOW_STEPS)[None, :], dist.shape).astype(np.int32))
    return np.stack(buckets), np.stack(prev_cols)


def _bias_kernel(rel_ref, bucket_ref, prev_ref, out_ref):
    bucket = bucket_ref[0]
    is_prev = prev_ref[0] != 0
    for h in range(N_HEADS):
        tab = jnp.full(bucket.shape, NEG_INF, F32)
        for b in range(N_BUCKETS):
            tab = jnp.where(bucket == b, rel_ref[b, h], tab)
        rows = slice((h % 2) * WINDOW_STEPS, (h % 2 + 1) * WINDOW_STEPS)
        out_ref[0, 0, h // 2, rows, :] = tab
        out_ref[0, 1, h // 2, rows, :] = jnp.where(is_prev, NEG_INF, tab)


def _bias_call(rel_bias):
    bucket, prev_cols = _bucket_tables()
    npat = len(DILATIONS)
    tab_spec = pl.BlockSpec((1, WINDOW_STEPS, 2 * WINDOW_STEPS), lambda p: (p, 0, 0))
    return pl.pallas_call(
        _bias_kernel,
        grid=(npat,),
        in_specs=[pl.BlockSpec(memory_space=pltpu.SMEM), tab_spec, tab_spec],
        out_specs=pl.BlockSpec((1, 2, N_PAIRS, 2 * WINDOW_STEPS, 2 * WINDOW_STEPS),
                               lambda p: (p, 0, 0, 0, 0)),
        out_shape=jax.ShapeDtypeStruct((npat, 2, N_PAIRS, 2 * WINDOW_STEPS, 2 * WINDOW_STEPS), F32),
        name="bias",
    )(rel_bias, jnp.asarray(bucket), jnp.asarray(prev_cols))


def _attn_kernel(q_ref, k_ref, v_ref, bias_ref, o_ref,
                 q32, k32, v32, acc_s, m_s, l_s, nat, *, supers_per_seq):
    bm = pl.program_id(1)
    is_first = bm % supers_per_seq == 0
    first = jnp.where(is_first, 1, 0)
    w = WINDOW_STEPS

    @pl.when(is_first)
    def _():
        k32[:, 0:w, :] = jnp.zeros((MAX_DIL, w, LANES), F32)
        v32[:, 0:w, :] = jnp.zeros((MAX_DIL, w, LANES), F32)

    @pl.when(jnp.logical_not(is_first))
    def _():
        k32[:, 0:w, :] = k32[:, w:2 * w, :]
        v32[:, 0:w, :] = v32[:, w:2 * w, :]

    nat[...] = q_ref[...].astype(F32)
    for r in range(MAX_DIL):
        q32[r] = nat[pl.ds(r, w, stride=MAX_DIL), :]
    nat[...] = k_ref[...].astype(F32)
    for r in range(MAX_DIL):
        k32[r, w:2 * w, :] = nat[pl.ds(r, w, stride=MAX_DIL), :]
    nat[...] = v_ref[...].astype(F32)
    for r in range(MAX_DIL):
        v32[r, w:2 * w, :] = nat[pl.ds(r, w, stride=MAX_DIL), :]

    head0 = lax.broadcasted_iota(jnp.int32, (w, LANES), 1) < HEAD_DIM

    def attend(p, slab0, ns, c, n, bias):
        step = MAX_DIL // ns
        rows_q = pl.ds(pl.multiple_of(c * n, 8), c)
        rows_k = pl.ds(pl.multiple_of(w - c + c * n, 8), 2 * c)
        if ns == 1:
            slabs = pl.ds(slab0, 1)
        else:
            slabs = pl.ds(slab0, ns, stride=step)
        q = q32[slabs, rows_q, :].reshape(w, LANES)
        kk = k32[slabs, rows_k, :].reshape(2 * w, LANES).astype(BF16)
        vv = v32[slabs, rows_k, :].reshape(2 * w, LANES).astype(BF16)
        zero = jnp.zeros_like(q)
        qm = jnp.concatenate([jnp.where(head0, q, zero), jnp.where(head0, zero, q)], axis=0).astype(BF16)
        s = lax.dot_general(qm, kk, (((1,), (1,)), ((), ())), preferred_element_type=F32) + bias
        m = jnp.max(s, axis=-1, keepdims=True)
        e = jnp.exp(s - m)
        l = jnp.sum(e, axis=-1, keepdims=True)
        pv = jnp.dot(e.astype(BF16), vv, preferred_element_type=F32)
        acc = jnp.where(head0, pv[:w], pv[w:])
        m_b = jnp.where(head0, m[:w], m[w:])
        l_b = jnp.where(head0, l[:w], l[w:])
        acc_s[p, slabs, rows_q, :] = acc.reshape(ns, c, LANES)
        m_s[p, slabs, rows_q, :] = m_b.reshape(ns, c, LANES)
        l_s[p, slabs, rows_q, :] = l_b.reshape(ns, c, LANES)

    def body16(r, carry):
        attend(2, r, 1, w, 0, bias_ref[2, first])
        return carry
    lax.fori_loop(0, MAX_DIL, body16, 0, unroll=True)

    for a in range(4):
        def body4(n, carry, a=a):
            attend(1, a, 4, 32, n, bias_ref[1, jnp.where(n == 0, first, 0)])
            return carry
        lax.fori_loop(0, 4, body4, 0, unroll=True)

    def body1(n, carry):
        attend(0, 0, 16, 8, n, bias_ref[0, jnp.where(n == 0, first, 0)])
        return carry
    lax.fori_loop(0, 16, body1, 0, unroll=True)

    def combine(r, carry):
        m0, m1, m2 = m_s[0, r], m_s[1, r], m_s[2, r]
        m_all = jnp.maximum(jnp.maximum(m0, m1), m2)
        w0, w1, w2 = jnp.exp(m0 - m_all), jnp.exp(m1 - m_all), jnp.exp(m2 - m_all)
        num = w0 * acc_s[0, r] + w1 * acc_s[1, r] + w2 * acc_s[2, r]
        den = w0 * l_s[0, r] + w1 * l_s[1, r] + w2 * l_s[2, r]
        nat[pl.ds(r, WINDOW_STEPS, stride=MAX_DIL), :] = num / den
        return carry
    lax.fori_loop(0, MAX_DIL, combine, 0)
    o_ref[...] = nat[...].astype(BF16)


def _attn_call(q, k, v, bias, batch, seq):
    n = q.shape[1]
    supers_per_seq = seq // SUPER
    w = WINDOW_STEPS
    tok = pl.BlockSpec((None, SUPER, LANES), lambda hp, bm: (hp, bm, 0))
    npat = len(DILATIONS)
    slab = lambda rws: pltpu.VMEM((MAX_DIL, rws, LANES), F32)
    part = pltpu.VMEM((npat, MAX_DIL, w, LANES), F32)
    return pl.pallas_call(
        functools.partial(_attn_kernel, supers_per_seq=supers_per_seq),
        grid=(N_PAIRS, n // SUPER),
        in_specs=[tok, tok, tok,
                  pl.BlockSpec((npat, 2, None, 2 * w, 2 * w), lambda hp, bm: (0, 0, hp, 0, 0))],
        out_specs=tok,
        out_shape=jax.ShapeDtypeStruct((N_PAIRS, n, LANES), BF16),
        scratch_shapes=[slab(w), slab(2 * w), slab(2 * w), part, part, part,
                        pltpu.VMEM((SUPER, LANES), F32)],
        compiler_params=pltpu.CompilerParams(dimension_semantics=("arbitrary", "arbitrary"),
                                             vmem_limit_bytes=VMEM_LIMIT),
        name="attn",
    )(q, k, v, bias)


def _mlp_kernel(x_ref, ypool_ref, yattn_ref, wo_ref, g_ref, wu_ref, wd_ref, o_ref):
    mixed = jnp.concatenate([ypool_ref[...]] + [yattn_ref[hp] for hp in range(N_PAIRS)], axis=-1)
    h = x_ref[...] + jnp.dot(mixed, wo_ref[...], preferred_element_type=F32)
    c = _rms(h, g_ref[...]).astype(BF16)
    acc = h
    for f in range(0, D_FF, FF_CHUNK):
        up = jnp.dot(c, wu_ref[:, f:f + FF_CHUNK], preferred_element_type=F32)
        ff = jnp.square(jnp.maximum(up, 0.0)).astype(BF16)
        acc = acc + jnp.dot(ff, wd_ref[f:f + FF_CHUNK, :], preferred_element_type=F32)
    o_ref[...] = acc


def _mlp_call(x2, ypool, yattn, w_out, mlp_g, w_up, w_down):
    n = x2.shape[0]
    tm = TM_MLP
    const = lambda shape: pl.BlockSpec(shape, lambda i: (0,) * len(shape),
                                       pipeline_mode=pl.Buffered(1))
    return pl.pallas_call(
        _mlp_kernel,
        grid=(n // tm,),
        in_specs=[
            pl.BlockSpec((tm, D_MODEL), lambda i: (i, 0)),
            pl.BlockSpec((tm, POOL_WIDTH), lambda i: (i, 0)),
            pl.BlockSpec((N_PAIRS, tm, LANES), lambda i: (0, i, 0)),
            const((D_MODEL, D_MODEL)),
            const((1, D_MODEL)),
            const((D_MODEL, D_FF)),
            const((D_FF, D_MODEL)),
        ],
        out_specs=pl.BlockSpec((tm, D_MODEL), lambda i: (i, 0)),
        out_shape=jax.ShapeDtypeStruct((n, D_MODEL), F32),
        compiler_params=pltpu.CompilerParams(dimension_semantics=("arbitrary",),
                                             vmem_limit_bytes=VMEM_LIMIT),
        name="mlp",
    )(x2, ypool, yattn, w_out, mlp_g, w_up, w_down)


def kernel(x, mix_norm_g, w_in, pool_w, pool_scale, q_norm_g, k_norm_g, rel_bias,
           w_out, mlp_norm_g, w_up, w_down):
    batch, seq, d = x.shape
    assert d == D_MODEL and seq % SUPER == 0 and seq % TM_PROJ == 0
    x2 = x.reshape(batch * seq, d)
    head_of = np.arange(MXU_DIM) // HEAD_DIM
    gsum = jnp.asarray(head_of[:, None] == head_of[None, :], BF16)
    tile_heads = lambda g: jnp.tile(g.astype(F32), N_HEADS).reshape(1, ATTN_WIDTH)

    ypool, q, k, v = _proj_call(
        x2, mix_norm_g.reshape(1, d), w_in.astype(BF16), gsum,
        tile_heads(q_norm_g), tile_heads(k_norm_g), pool_w.astype(BF16),
        pool_scale.reshape(1, POOL_WIDTH), seq)
    bias = _bias_call(rel_bias)
    yattn = _attn_call(q, k, v, bias, batch, seq)
    y = _mlp_call(x2, ypool, yattn, w_out.astype(BF16), mlp_norm_g.reshape(1, d),
                  w_up.astype(BF16), w_down.astype(BF16))
    return y.reshape(batch, seq, d)
```

```python
import functools
import math

import numpy as np
import jax
import jax.numpy as jnp
from jax import lax
from jax.experimental import pallas as pl
from jax.experimental.pallas import tpu as pltpu

D_MODEL = 1024
POOL_WIDTH = 512
POOL_WINDOWS = (2, 4, 8, 16)
POOL_GROUP_DIM = 128
ATTN_WIDTH = 512
HEAD_DIM = 64
N_HEADS = 8
DILATIONS = (1, 4, 16)
WINDOW_STEPS = 128
N_BUCKETS = 32
MAX_DISTANCE = 2048
D_FF = 4096
NORM_EPS = 1e-6
NEG_INF = -1e30

LANES = 128
N_PAIRS = ATTN_WIDTH // LANES
MXU_DIM = 256
MAX_DIL = 16
SUPER = WINDOW_STEPS * MAX_DIL
HALO = 16
TM_PROJ = 512
TM_MLP = 512
FF_CHUNK = 1024
VMEM_LIMIT = 56 * 1024 * 1024

F32 = jnp.float32
BF16 = jnp.bfloat16


def _rms(x, g):
    return x * lax.rsqrt(jnp.mean(x * x, axis=-1, keepdims=True) + NORM_EPS) * g


def _proj_kernel(x_ref, g_ref, w_ref, gsum_ref, qg_ref, kg_ref, pw_ref, ps_ref,
                 ypool_ref, q_ref, k_ref, v_ref, ubuf, *, tiles_per_seq):
    i = pl.program_id(0)
    tm = x_ref.shape[0]

    seq_tile = i % tiles_per_seq

    @pl.when(seq_tile == 0)
    def _():
        ubuf[0:HALO, :] = jnp.zeros((HALO, POOL_WIDTH), F32)

    @pl.when(seq_tile != 0)
    def _():
        ubuf[0:HALO, :] = ubuf[tm:tm + HALO, :]

    a = _rms(x_ref[...], g_ref[...]).astype(BF16)

    def head_norm(t, gain):
        sq = t * t
        hi = sq.astype(BF16)
        lo = (sq - hi.astype(F32)).astype(BF16)
        gw = gsum_ref.shape[0]
        ssq = jnp.concatenate(
            [jnp.dot(hi[:, c:c + gw], gsum_ref[...], preferred_element_type=F32)
             + jnp.dot(lo[:, c:c + gw], gsum_ref[...], preferred_element_type=F32)
             for c in range(0, ATTN_WIDTH, gw)], axis=-1)
        return t * lax.rsqrt(ssq * (1.0 / HEAD_DIM) + NORM_EPS) * gain

    def put_pairs(ref, t):
        for hp in range(N_PAIRS):
            tt = pltpu.einshape("(jr)l->rjl", t[:, hp * LANES:(hp + 1) * LANES], r=MAX_DIL)
            for r in range(MAX_DIL):
                ref[hp, :, r * LANES:(r + 1) * LANES] = tt[r].astype(BF16)

    u = jnp.dot(a, w_ref[:, :POOL_WIDTH], preferred_element_type=F32)
    ubuf[HALO:HALO + tm, :] = u
    pos = seq_tile * tm + lax.broadcasted_iota(jnp.int32, (tm, 1), 0)
    for g, w in enumerate(POOL_WINDOWS):
        cols = slice(g * POOL_GROUP_DIM, (g + 1) * POOL_GROUP_DIM)
        wsum = ubuf[HALO:HALO + tm, cols]
        for d in range(1, w):
            wsum = wsum + ubuf[HALO - d:HALO - d + tm, cols]
        inv_count = 1.0 / jnp.minimum(pos + 1, w).astype(F32)
        pooled = wsum * inv_count - ubuf[HALO:HALO + tm, cols]
        mixed = jnp.dot(pooled.astype(BF16), pw_ref[g], preferred_element_type=F32)
        ypool_ref[:, cols] = (mixed * ps_ref[:, cols]).astype(BF16)

    q = jnp.dot(a, w_ref[:, POOL_WIDTH:POOL_WIDTH + ATTN_WIDTH], preferred_element_type=F32)
    put_pairs(q_ref, head_norm(q, qg_ref[...]) * (HEAD_DIM ** -0.5))
    k = jnp.dot(a, w_ref[:, POOL_WIDTH + ATTN_WIDTH:POOL_WIDTH + 2 * ATTN_WIDTH],
                preferred_element_type=F32)
    put_pairs(k_ref, head_norm(k, kg_ref[...]))
    v = jnp.dot(a, w_ref[:, POOL_WIDTH + 2 * ATTN_WIDTH:], preferred_element_type=F32)
    put_pairs(v_ref, v)


def _proj_call(x2, mix_g, w_in, gsum, qg, kg, pool_w, pool_scale, seq):
    n = x2.shape[0]
    tm = TM_PROJ
    const = lambda shape: pl.BlockSpec(shape, lambda i: (0,) * len(shape))
    pair_spec = pl.BlockSpec((N_PAIRS, tm // MAX_DIL, MAX_DIL * LANES), lambda i: (0, i, 0))
    pair_shape = jax.ShapeDtypeStruct((N_PAIRS, n // MAX_DIL, MAX_DIL * LANES), BF16)
    return pl.pallas_call(
        functools.partial(_proj_kernel, tiles_per_seq=seq // tm),
        grid=(n // tm,),
        in_specs=[
            pl.BlockSpec((tm, D_MODEL), lambda i: (i, 0)),
            const((1, D_MODEL)),
            const((D_MODEL, POOL_WIDTH + 3 * ATTN_WIDTH)),
            const((MXU_DIM, MXU_DIM)),
            const((1, ATTN_WIDTH)),
            const((1, ATTN_WIDTH)),
            const((len(POOL_WINDOWS), POOL_GROUP_DIM, POOL_GROUP_DIM)),
            const((1, POOL_WIDTH)),
        ],
        out_specs=[pl.BlockSpec((tm, POOL_WIDTH), lambda i: (i, 0)), pair_spec, pair_spec, pair_spec],
        out_shape=[jax.ShapeDtypeStruct((n, POOL_WIDTH), BF16), pair_shape, pair_shape, pair_shape],
        scratch_shapes=[pltpu.VMEM((HALO + tm, POOL_WIDTH), F32)],
        compiler_params=pltpu.CompilerParams(dimension_semantics=("arbitrary",),
                                             vmem_limit_bytes=VMEM_LIMIT),
        name="proj",
    )(x2, mix_g, w_in, gsum, qg, kg, pool_w, pool_scale)


def _block_orders():
    out = []
    for dil in DILATIONS:
        ns = MAX_DIL // dil
        c = WINDOW_STEPS // ns
        slab = np.arange(ns)[:, None]
        q_step = (np.arange(c)[None, :] * ns + slab + WINDOW_STEPS).reshape(-1)
        k_step = np.concatenate([q_step - WINDOW_STEPS, q_step])
        out.append((dil, ns, c, q_step, k_step))
    return out


def _bucket_tables():
    max_exact = N_BUCKETS // 2
    buckets, prev_cols = [], []
    for dil, ns, c, q_step, k_step in _block_orders():
        dist = q_step[:, None] - k_step[None, :]
        ok = (dist >= 0) & (dist <= WINDOW_STEPS)
        tok = np.clip(dist, 0, WINDOW_STEPS) * dil
        d_f = np.maximum(tok, 1).astype(np.float32)
        large = max_exact + (np.log(d_f / np.float32(max_exact)) / np.float32(math.log(MAX_DISTANCE / max_exact))
                             * np.float32(N_BUCKETS - max_exact)).astype(np.int32)
        large = np.minimum(large, N_BUCKETS - 1)
        bucket = np.where(tok < max_exact, tok, large)
        buckets.append(np.where(ok, bucket, -1).astype(np.int32))
        prev_cols.append(np.broadcast_to((k_step < WINDOW_STEPS)[None, :], dist.shape).astype(np.int32))
    return np.stack(buckets), np.stack(prev_cols)


def _bias_kernel(rel_ref, bucket_ref, prev_ref, out_ref):
    bucket = bucket_ref[0]
    is_prev = prev_ref[0] != 0
    for h in range(N_HEADS):
        tab = jnp.full(bucket.shape, NEG_INF, F32)
        for b in range(N_BUCKETS):
            tab = jnp.where(bucket == b, rel_ref[b, h], tab)
        rows = slice((h % 2) * WINDOW_STEPS, (h % 2 + 1) * WINDOW_STEPS)
        out_ref[0, 0, h // 2, rows, :] = tab
        out_ref[0, 1, h // 2, rows, :] = jnp.where(is_prev, NEG_INF, tab)


def _bias_call(rel_bias):
    bucket, prev_cols = _bucket_tables()
    npat = len(DILATIONS)
    tab_spec = pl.BlockSpec((1, WINDOW_STEPS, 2 * WINDOW_STEPS), lambda p: (p, 0, 0))
    return pl.pallas_call(
        _bias_kernel,
        grid=(npat,),
        in_specs=[pl.BlockSpec(memory_space=pltpu.SMEM), tab_spec, tab_spec],
        out_specs=pl.BlockSpec((1, 2, N_PAIRS, 2 * WINDOW_STEPS, 2 * WINDOW_STEPS),
                               lambda p: (p, 0, 0, 0, 0)),
        out_shape=jax.ShapeDtypeStruct((npat, 2, N_PAIRS, 2 * WINDOW_STEPS, 2 * WINDOW_STEPS), F32),
        name="bias",
    )(rel_bias, jnp.asarray(bucket), jnp.asarray(prev_cols))


CARRY0 = MAX_DIL


def _pattern_blocks(p):
    dil = DILATIONS[p]
    ns = MAX_DIL // dil
    c = WINDOW_STEPS // ns
    blocks = []
    for a in range(dil):
        slabs = [a + dil * j for j in range(ns)]
        for n in range(ns):
            b = a * ns + n
            blocks.append((b, slabs, c * n, c, b - 1 if n else CARRY0 + a, n == 0))
    return blocks


def _attn_kernel(q_ref, k_ref, v_ref, bias_ref, o_ref,
                 qa32, qb32, k32, v32, kc_s, vc_s, acc_s, m_s, l_s, *, supers_per_seq):
    bm = pl.program_id(1)
    is_first = bm % supers_per_seq == 0
    first = jnp.where(is_first, 1, 0)
    w = WINDOW_STEPS
    npat = len(DILATIONS)

    @pl.when(is_first)
    def _():
        for p in range(npat):
            for a in range(DILATIONS[p]):
                kc_s[p, CARRY0 + a] = jnp.zeros((w, LANES), BF16)
                vc_s[p, CARRY0 + a] = jnp.zeros((w, LANES), BF16)

    @pl.when(jnp.logical_not(is_first))
    def _():
        for p in range(npat):
            ns = MAX_DIL // DILATIONS[p]
            for a in range(DILATIONS[p]):
                kc_s[p, CARRY0 + a] = kc_s[p, a * ns + ns - 1]
                vc_s[p, CARRY0 + a] = vc_s[p, a * ns + ns - 1]

    head0 = lax.broadcasted_iota(jnp.int32, (w, LANES), 1) < HEAD_DIM
    for r in range(MAX_DIL):
        cols = slice(r * LANES, (r + 1) * LANES)
        q = q_ref[:, cols].astype(F32)
        zero = jnp.zeros_like(q)
        qa32[r] = jnp.where(head0, q, zero)
        qb32[r] = jnp.where(head0, zero, q)
        k32[r] = k_ref[:, cols].astype(F32)
        v32[r] = v_ref[:, cols].astype(F32)

    def gather(ref, slabs, row0, c):
        return jnp.concatenate([ref[s, row0:row0 + c, :] for s in slabs], axis=0)

    def attend(p, blk):
        b, slabs, row0, c, prev, head = blk
        k_cur = gather(k32, slabs, row0, c).astype(BF16)
        v_cur = gather(v32, slabs, row0, c).astype(BF16)
        qm = jnp.concatenate([gather(qa32, slabs, row0, c), gather(qb32, slabs, row0, c)], axis=0).astype(BF16)
        kk = jnp.concatenate([kc_s[p, prev], k_cur], axis=0)
        vv = jnp.concatenate([vc_s[p, prev], v_cur], axis=0)
        kc_s[p, b] = k_cur
        vc_s[p, b] = v_cur
        bias = bias_ref[p, first] if head else bias_ref[p, 0]
        s = lax.dot_general(qm, kk, (((1,), (1,)), ((), ())), preferred_element_type=F32) + bias
        m = jnp.max(s, axis=-1, keepdims=True)
        e = jnp.exp(s - m)
        l = jnp.sum(e, axis=-1, keepdims=True)
        pv = jnp.dot(e.astype(BF16), vv, preferred_element_type=F32)
        acc = jnp.where(head0, pv[:w], pv[w:])
        m_b = jnp.where(head0, m[:w], m[w:])
        l_b = jnp.where(head0, l[:w], l[w:])
        for j, sl in enumerate(slabs):
            acc_s[p, sl, row0:row0 + c, :] = acc[j * c:(j + 1) * c, :]
            m_s[p, sl, row0:row0 + c, :] = m_b[j * c:(j + 1) * c, :]
            l_s[p, sl, row0:row0 + c, :] = l_b[j * c:(j + 1) * c, :]

    for p in reversed(range(npat)):
        for blk in _pattern_blocks(p):
            attend(p, blk)

    for r in range(MAX_DIL):
        m0, m1, m2 = m_s[0, r], m_s[1, r], m_s[2, r]
        m_all = jnp.maximum(jnp.maximum(m0, m1), m2)
        w0, w1, w2 = jnp.exp(m0 - m_all), jnp.exp(m1 - m_all), jnp.exp(m2 - m_all)
        num = w0 * acc_s[0, r] + w1 * acc_s[1, r] + w2 * acc_s[2, r]
        den = w0 * l_s[0, r] + w1 * l_s[1, r] + w2 * l_s[2, r]
        o_ref[:, r * LANES:(r + 1) * LANES] = (num / den).astype(BF16)


def _attn_call(q, k, v, bias, batch, seq):
    rows, width = q.shape[1:]
    supers_per_seq = seq // SUPER
    w = WINDOW_STEPS
    tok = pl.BlockSpec((None, w, width), lambda hp, bm: (hp, bm, 0))
    npat = len(DILATIONS)
    slab = pltpu.VMEM((MAX_DIL, w, LANES), F32)
    part = pltpu.VMEM((npat, MAX_DIL, w, LANES), F32)
    chunks = pltpu.VMEM((npat, 2 * MAX_DIL, w, LANES), BF16)
    return pl.pallas_call(
        functools.partial(_attn_kernel, supers_per_seq=supers_per_seq),
        grid=(N_PAIRS, rows // w),
        in_specs=[tok, tok, tok,
                  pl.BlockSpec((npat, 2, None, 2 * w, 2 * w), lambda hp, bm: (0, 0, hp, 0, 0))],
        out_specs=tok,
        out_shape=jax.ShapeDtypeStruct((N_PAIRS, rows, width), BF16),
        scratch_shapes=[slab, slab, slab, slab, chunks, chunks, part, part, part],
        compiler_params=pltpu.CompilerParams(dimension_semantics=("arbitrary", "arbitrary"),
                                             vmem_limit_bytes=VMEM_LIMIT),
        name="attn",
    )(q, k, v, bias)


def _mlp_kernel(x_ref, ypool_ref, yattn_ref, wo_ref, g_ref, wu_ref, wd_ref, o_ref, relay):
    tm = x_ref.shape[0]
    for hp in range(N_PAIRS):
        for r in range(MAX_DIL):
            relay[hp, pl.ds(r, tm // MAX_DIL, stride=MAX_DIL), :] = (
                yattn_ref[hp, :, r * LANES:(r + 1) * LANES].astype(F32))
    mixed = jnp.concatenate([ypool_ref[...]] + [relay[hp].astype(BF16) for hp in range(N_PAIRS)], axis=-1)
    h = x_ref[...] + jnp.dot(mixed, wo_ref[...], preferred_element_type=F32)
    c = _rms(h, g_ref[...]).astype(BF16)
    acc = h
    for f in range(0, D_FF, FF_CHUNK):
        up = jnp.dot(c, wu_ref[:, f:f + FF_CHUNK], preferred_element_type=F32)
        ff = jnp.square(jnp.maximum(up, 0.0)).astype(BF16)
        acc = acc + jnp.dot(ff, wd_ref[f:f + FF_CHUNK, :], preferred_element_type=F32)
    o_ref[...] = acc


def _mlp_call(x2, ypool, yattn, w_out, mlp_g, w_up, w_down):
    n = x2.shape[0]
    tm = TM_MLP
    const = lambda shape: pl.BlockSpec(shape, lambda i: (0,) * len(shape),
                                       pipeline_mode=pl.Buffered(1))
    return pl.pallas_call(
        _mlp_kernel,
        grid=(n // tm,),
        in_specs=[
            pl.BlockSpec((tm, D_MODEL), lambda i: (i, 0)),
            pl.BlockSpec((tm, POOL_WIDTH), lambda i: (i, 0)),
            pl.BlockSpec((N_PAIRS, tm // MAX_DIL, MAX_DIL * LANES), lambda i: (0, i, 0)),
            const((D_MODEL, D_MODEL)),
            const((1, D_MODEL)),
            const((D_MODEL, D_FF)),
            const((D_FF, D_MODEL)),
        ],
        out_specs=pl.BlockSpec((tm, D_MODEL), lambda i: (i, 0)),
        out_shape=jax.ShapeDtypeStruct((n, D_MODEL), F32),
        scratch_shapes=[pltpu.VMEM((N_PAIRS, tm, LANES), F32)],
        compiler_params=pltpu.CompilerParams(dimension_semantics=("arbitrary",),
                                             vmem_limit_bytes=VMEM_LIMIT),
        name="mlp",
    )(x2, ypool, yattn, w_out, mlp_g, w_up, w_down)


def kernel(x, mix_norm_g, w_in, pool_w, pool_scale, q_norm_g, k_norm_g, rel_bias,
           w_out, mlp_norm_g, w_up, w_down):
    batch, seq, d = x.shape
    assert d == D_MODEL and seq % SUPER == 0 and seq % TM_PROJ == 0
    x2 = x.reshape(batch * seq, d)
    head_of = np.arange(MXU_DIM) // HEAD_DIM
    gsum = jnp.asarray(head_of[:, None] == head_of[None, :], BF16)
    tile_heads = lambda g: jnp.tile(g.astype(F32), N_HEADS).reshape(1, ATTN_WIDTH)

    ypool, q, k, v = _proj_call(
        x2, mix_norm_g.reshape(1, d), w_in.astype(BF16), gsum,
        tile_heads(q_norm_g), tile_heads(k_norm_g), pool_w.astype(BF16),
        pool_scale.reshape(1, POOL_WIDTH), seq)
    bias = _bias_call(rel_bias)
    yattn = _attn_call(q, k, v, bias, batch, seq)
    y = _mlp_call(x2, ypool, yattn, w_out.astype(BF16), mlp_norm_g.reshape(1, d),
                  w_up.astype(BF16), w_down.astype(BF16))
    return y.reshape(batch, seq, d)
```

```python
import functools
import math

import numpy as np
import jax
import jax.numpy as jnp
from jax import lax
from jax.experimental import pallas as pl
from jax.experimental.pallas import tpu as pltpu

D_MODEL = 1024
POOL_WIDTH = 512
POOL_WINDOWS = (2, 4, 8, 16)
POOL_GROUP_DIM = 128
ATTN_WIDTH = 512
HEAD_DIM = 64
N_HEADS = 8
DILATIONS = (1, 4, 16)
WINDOW_STEPS = 128
N_BUCKETS = 32
MAX_DISTANCE = 2048
D_FF = 4096
NORM_EPS = 1e-6
NEG_INF = -1e30

LANES = 128
N_PAIRS = ATTN_WIDTH // LANES
MXU_DIM = 256
MAX_DIL = 16
SUPER = WINDOW_STEPS * MAX_DIL
HALO = 32
TM_PROJ = 1024
TM_MLP = 512
FF_CHUNK = 1024
VMEM_LIMIT = 56 * 1024 * 1024

F32 = jnp.float32
BF16 = jnp.bfloat16


def _rms(x, g):
    return x * lax.rsqrt(jnp.mean(x * x, axis=-1, keepdims=True) + NORM_EPS) * g


def _proj_kernel(x_ref, g_ref, w_ref, gsum_ref, qg_ref, kg_ref, pw_ref, ps_ref,
                 ypool_ref, q_ref, k_ref, v_ref, ubuf, sbuf, *, tiles_per_seq):
    i = pl.program_id(0)
    tm = x_ref.shape[0]

    seq_tile = i % tiles_per_seq

    @pl.when(seq_tile == 0)
    def _():
        ubuf[0:HALO, :] = jnp.zeros((HALO, POOL_WIDTH), F32)

    @pl.when(seq_tile != 0)
    def _():
        ubuf[0:HALO, :] = ubuf[tm:tm + HALO, :]

    a = _rms(x_ref[...], g_ref[...]).astype(BF16)

    def head_norm(t, gain):
        sq = t * t
        hi = sq.astype(BF16)
        lo = (sq - hi.astype(F32)).astype(BF16)
        gw = gsum_ref.shape[0]
        ssq = jnp.concatenate(
            [jnp.dot(hi[:, c:c + gw], gsum_ref[...], preferred_element_type=F32)
             + jnp.dot(lo[:, c:c + gw], gsum_ref[...], preferred_element_type=F32)
             for c in range(0, ATTN_WIDTH, gw)], axis=-1)
        return t * lax.rsqrt(ssq + NORM_EPS) * gain

    def put_pairs(ref, t):
        for hp in range(N_PAIRS):
            tt = pltpu.einshape("(jr)l->rjl", t[:, hp * LANES:(hp + 1) * LANES], r=MAX_DIL)
            for r in range(MAX_DIL):
                ref[hp, :, r * LANES:(r + 1) * LANES] = tt[r].astype(BF16)

    u = jnp.dot(a, w_ref[:, :POOL_WIDTH], preferred_element_type=F32)
    ubuf[HALO:HALO + tm, :] = u
    pos = seq_tile * tm + lax.broadcasted_iota(jnp.int32, (tm, 1), 0)
    pooled = []
    end = HALO + tm
    for g, w in enumerate(POOL_WINDOWS):
        cols = slice(g * POOL_GROUP_DIM, (g + 1) * POOL_GROUP_DIM)
        levels = int(math.log2(w))
        src, span = ubuf, 1
        for lvl in range(levels):
            lo = HALO if lvl == levels - 1 else HALO - 8 * (levels - 1 - lvl)
            summed = src[lo:end, cols] + src[lo - span:end - span, cols]
            span *= 2
            if lvl == levels - 1:
                wsum = summed
            else:
                sbuf[lvl, lo:end, cols] = summed
                src = sbuf.at[lvl]
        inv_count = 1.0 / jnp.minimum(pos + 1, w).astype(F32)
        pooled.append((wsum * inv_count - ubuf[HALO:end, cols]).astype(BF16))

    q = jnp.dot(a, w_ref[:, POOL_WIDTH:POOL_WIDTH + ATTN_WIDTH], preferred_element_type=F32)
    put_pairs(q_ref, head_norm(q, qg_ref[...]))
    k = jnp.dot(a, w_ref[:, POOL_WIDTH + ATTN_WIDTH:POOL_WIDTH + 2 * ATTN_WIDTH],
                preferred_element_type=F32)
    put_pairs(k_ref, head_norm(k, kg_ref[...]))
    v = jnp.dot(a, w_ref[:, POOL_WIDTH + 2 * ATTN_WIDTH:], preferred_element_type=F32)
    put_pairs(v_ref, v)

    for g in range(len(POOL_WINDOWS)):
        cols = slice(g * POOL_GROUP_DIM, (g + 1) * POOL_GROUP_DIM)
        mixed = jnp.dot(pooled[g], pw_ref[g], preferred_element_type=F32)
        ypool_ref[:, cols] = (mixed * ps_ref[:, cols]).astype(BF16)


def _proj_call(x2, mix_g, w_in, gsum, qg, kg, pool_w, pool_scale, seq):
    n = x2.shape[0]
    tm = TM_PROJ
    const = lambda shape: pl.BlockSpec(shape, lambda i: (0,) * len(shape))
    pair_spec = pl.BlockSpec((N_PAIRS, tm // MAX_DIL, MAX_DIL * LANES), lambda i: (0, i, 0))
    pair_shape = jax.ShapeDtypeStruct((N_PAIRS, n // MAX_DIL, MAX_DIL * LANES), BF16)
    return pl.pallas_call(
        functools.partial(_proj_kernel, tiles_per_seq=seq // tm),
        grid=(n // tm,),
        in_specs=[
            pl.BlockSpec((tm, D_MODEL), lambda i: (i, 0)),
            const((1, D_MODEL)),
            const((D_MODEL, POOL_WIDTH + 3 * ATTN_WIDTH)),
            const((MXU_DIM, MXU_DIM)),
            const((1, ATTN_WIDTH)),
            const((1, ATTN_WIDTH)),
            const((len(POOL_WINDOWS), POOL_GROUP_DIM, POOL_GROUP_DIM)),
            const((1, POOL_WIDTH)),
        ],
        out_specs=[pl.BlockSpec((tm, POOL_WIDTH), lambda i: (i, 0)), pair_spec, pair_spec, pair_spec],
        out_shape=[jax.ShapeDtypeStruct((n, POOL_WIDTH), BF16), pair_shape, pair_shape, pair_shape],
        scratch_shapes=[pltpu.VMEM((HALO + tm, POOL_WIDTH), F32),
                        pltpu.VMEM((3, HALO + tm, POOL_WIDTH), F32)],
        compiler_params=pltpu.CompilerParams(dimension_semantics=("arbitrary",),
                                             vmem_limit_bytes=VMEM_LIMIT),
        name="proj",
    )(x2, mix_g, w_in, gsum, qg, kg, pool_w, pool_scale)


def _block_orders():
    out = []
    for dil in DILATIONS:
        ns = MAX_DIL // dil
        c = WINDOW_STEPS // ns
        slab = np.arange(ns)[:, None]
        q_step = (np.arange(c)[None, :] * ns + slab + WINDOW_STEPS).reshape(-1)
        k_step = np.concatenate([q_step - WINDOW_STEPS, q_step])
        out.append((dil, ns, c, q_step, k_step))
    return out


def _bucket_tables():
    max_exact = N_BUCKETS // 2
    buckets, prev_cols = [], []
    for dil, ns, c, q_step, k_step in _block_orders():
        dist = q_step[:, None] - k_step[None, :]
        ok = (dist >= 0) & (dist <= WINDOW_STEPS)
        tok = np.clip(dist, 0, WINDOW_STEPS) * dil
        d_f = np.maximum(tok, 1).astype(np.float32)
        large = max_exact + (np.log(d_f / np.float32(max_exact)) / np.float32(math.log(MAX_DISTANCE / max_exact))
                             * np.float32(N_BUCKETS - max_exact)).astype(np.int32)
        large = np.minimum(large, N_BUCKETS - 1)
        bucket = np.where(tok < max_exact, tok, large)
        buckets.append(np.where(ok, bucket, -1).astype(np.int32))
        prev_cols.append(np.broadcast_to((k_step < WINDOW_STEPS)[None, :], dist.shape).astype(np.int32))
    return np.stack(buckets), np.stack(prev_cols)


def _bias_kernel(rel_ref, bucket_ref, prev_ref, out_ref):
    bucket = bucket_ref[0]
    is_prev = prev_ref[0] != 0
    for h in range(N_HEADS):
        tab = jnp.full(bucket.shape, NEG_INF, F32)
        for b in range(N_BUCKETS):
            tab = jnp.where(bucket == b, rel_ref[b, h], tab)
        rows = slice((h % 2) * WINDOW_STEPS, (h % 2 + 1) * WINDOW_STEPS)
        out_ref[0, 0, h // 2, rows, :] = tab
        out_ref[0, 1, h // 2, rows, :] = jnp.where(is_prev, NEG_INF, tab)


def _bias_call(rel_bias):
    bucket, prev_cols = _bucket_tables()
    npat = len(DILATIONS)
    tab_spec = pl.BlockSpec((1, WINDOW_STEPS, 2 * WINDOW_STEPS), lambda p: (p, 0, 0))
    return pl.pallas_call(
        _bias_kernel,
        grid=(npat,),
        in_specs=[pl.BlockSpec(memory_space=pltpu.SMEM), tab_spec, tab_spec],
        out_specs=pl.BlockSpec((1, 2, N_PAIRS, 2 * WINDOW_STEPS, 2 * WINDOW_STEPS),
                               lambda p: (p, 0, 0, 0, 0)),
        out_shape=jax.ShapeDtypeStruct((npat, 2, N_PAIRS, 2 * WINDOW_STEPS, 2 * WINDOW_STEPS), F32),
        name="bias",
    )(rel_bias, jnp.asarray(bucket), jnp.asarray(prev_cols))


CARRY0 = MAX_DIL


def _pattern_blocks(p):
    dil = DILATIONS[p]
    ns = MAX_DIL // dil
    c = WINDOW_STEPS // ns
    blocks = []
    for a in range(dil):
        slabs = [a + dil * j for j in range(ns)]
        for n in range(ns):
            b = a * ns + n
            blocks.append((b, slabs, c * n, c, b - 1 if n else CARRY0 + a, n == 0))
    return blocks


def _attn_kernel(q_ref, k_ref, v_ref, bias_ref, o_ref,
                 qa32, qb32, k32, v32, kc_s, vc_s, acc_s, m_s, l_s, *, supers_per_seq):
    bm = pl.program_id(1)
    is_first = bm % supers_per_seq == 0
    first = jnp.where(is_first, 1, 0)
    w = WINDOW_STEPS
    npat = len(DILATIONS)

    @pl.when(is_first)
    def _():
        for p in range(npat):
            for a in range(DILATIONS[p]):
                kc_s[p, CARRY0 + a] = jnp.zeros((w, LANES), BF16)
                vc_s[p, CARRY0 + a] = jnp.zeros((w, LANES), BF16)

    @pl.when(jnp.logical_not(is_first))
    def _():
        for p in range(npat):
            ns = MAX_DIL // DILATIONS[p]
            for a in range(DILATIONS[p]):
                kc_s[p, CARRY0 + a] = kc_s[p, a * ns + ns - 1]
                vc_s[p, CARRY0 + a] = vc_s[p, a * ns + ns - 1]

    head0 = lax.broadcasted_iota(jnp.int32, (w, LANES), 1) < HEAD_DIM
    for r in range(MAX_DIL):
        cols = slice(r * LANES, (r + 1) * LANES)
        q = q_ref[:, cols].astype(F32)
        zero = jnp.zeros_like(q)
        qa32[r] = jnp.where(head0, q, zero)
        qb32[r] = jnp.where(head0, zero, q)
        k32[r] = k_ref[:, cols].astype(F32)
        v32[r] = v_ref[:, cols].astype(F32)

    def gather(ref, slabs, row0, c):
        return jnp.concatenate([ref[s, row0:row0 + c, :] for s in slabs], axis=0)

    def attend(p, blk):
        b, slabs, row0, c, prev, head = blk
        k_cur = gather(k32, slabs, row0, c).astype(BF16)
        v_cur = gather(v32, slabs, row0, c).astype(BF16)
        qm = jnp.concatenate([gather(qa32, slabs, row0, c), gather(qb32, slabs, row0, c)], axis=0).astype(BF16)
        kk = jnp.concatenate([kc_s[p, prev], k_cur], axis=0)
        vv = jnp.concatenate([vc_s[p, prev], v_cur], axis=0)
        kc_s[p, b] = k_cur
        vc_s[p, b] = v_cur
        bias = bias_ref[p, first] if head else bias_ref[p, 0]
        s = lax.dot_general(qm, kk, (((1,), (1,)), ((), ())), preferred_element_type=F32) + bias
        m = jnp.max(s, axis=-1, keepdims=True)
        e = jnp.exp(s - m)
        l = jnp.sum(e, axis=-1, keepdims=True)
        pv = jnp.dot(e.astype(BF16), vv, preferred_element_type=F32)
        acc = jnp.where(head0, pv[:w], pv[w:])
        m_b = jnp.where(head0, m[:w], m[w:])
        l_b = jnp.where(head0, l[:w], l[w:])
        for j, sl in enumerate(slabs):
            acc_s[p, sl, row0:row0 + c, :] = acc[j * c:(j + 1) * c, :]
            m_s[p, sl, row0:row0 + c, :] = m_b[j * c:(j + 1) * c, :]
            l_s[p, sl, row0:row0 + c, :] = l_b[j * c:(j + 1) * c, :]

    for p in reversed(range(npat)):
        for blk in _pattern_blocks(p):
            attend(p, blk)

    for r in range(MAX_DIL):
        m0, m1, m2 = m_s[0, r], m_s[1, r], m_s[2, r]
        m_all = jnp.maximum(jnp.maximum(m0, m1), m2)
        w0, w1, w2 = jnp.exp(m0 - m_all), jnp.exp(m1 - m_all), jnp.exp(m2 - m_all)
        num = w0 * acc_s[0, r] + w1 * acc_s[1, r] + w2 * acc_s[2, r]
        den = w0 * l_s[0, r] + w1 * l_s[1, r] + w2 * l_s[2, r]
        o_ref[:, r * LANES:(r + 1) * LANES] = (num / den).astype(BF16)


def _attn_call(q, k, v, bias, batch, seq):
    rows, width = q.shape[1:]
    supers_per_seq = seq // SUPER
    w = WINDOW_STEPS
    tok = pl.BlockSpec((None, w, width), lambda hp, bm: (hp, bm, 0))
    npat = len(DILATIONS)
    slab = pltpu.VMEM((MAX_DIL, w, LANES), F32)
    part = pltpu.VMEM((npat, MAX_DIL, w, LANES), F32)
    chunks = pltpu.VMEM((npat, 2 * MAX_DIL, w, LANES), BF16)
    return pl.pallas_call(
        functools.partial(_attn_kernel, supers_per_seq=supers_per_seq),
        grid=(N_PAIRS, rows // w),
        in_specs=[tok, tok, tok,
                  pl.BlockSpec((npat, 2, None, 2 * w, 2 * w), lambda hp, bm: (0, 0, hp, 0, 0))],
        out_specs=tok,
        out_shape=jax.ShapeDtypeStruct((N_PAIRS, rows, width), BF16),
        scratch_shapes=[slab, slab, slab, slab, chunks, chunks, part, part, part],
        compiler_params=pltpu.CompilerParams(dimension_semantics=("arbitrary", "arbitrary"),
                                             vmem_limit_bytes=VMEM_LIMIT),
        name="attn",
    )(q, k, v, bias)


def _mlp_kernel(x_ref, ypool_ref, yattn_ref, wo_ref, g_ref, wu_ref, wd_ref, o_ref, relay):
    tm = x_ref.shape[0]
    for hp in range(N_PAIRS):
        for r in range(MAX_DIL):
            relay[hp, pl.ds(r, tm // MAX_DIL, stride=MAX_DIL), :] = (
                yattn_ref[hp, :, r * LANES:(r + 1) * LANES].astype(F32))
    mixed = jnp.concatenate([ypool_ref[...]] + [relay[hp].astype(BF16) for hp in range(N_PAIRS)], axis=-1)
    h = x_ref[...] + jnp.dot(mixed, wo_ref[...], preferred_element_type=F32)
    c = _rms(h, g_ref[...]).astype(BF16)
    acc = h
    for f in range(0, D_FF, FF_CHUNK):
        up = jnp.dot(c, wu_ref[:, f:f + FF_CHUNK], preferred_element_type=F32)
        ff = jnp.square(jnp.maximum(up, 0.0)).astype(BF16)
        acc = acc + jnp.dot(ff, wd_ref[f:f + FF_CHUNK, :], preferred_element_type=F32)
    o_ref[...] = acc


def _mlp_call(x2, ypool, yattn, w_out, mlp_g, w_up, w_down):
    n = x2.shape[0]
    tm = TM_MLP
    const = lambda shape: pl.BlockSpec(shape, lambda i: (0,) * len(shape),
                                       pipeline_mode=pl.Buffered(1))
    return pl.pallas_call(
        _mlp_kernel,
        grid=(n // tm,),
        in_specs=[
            pl.BlockSpec((tm, D_MODEL), lambda i: (i, 0)),
            pl.BlockSpec((tm, POOL_WIDTH), lambda i: (i, 0)),
            pl.BlockSpec((N_PAIRS, tm // MAX_DIL, MAX_DIL * LANES), lambda i: (0, i, 0)),
            const((D_MODEL, D_MODEL)),
            const((1, D_MODEL)),
            const((D_MODEL, D_FF)),
            const((D_FF, D_MODEL)),
        ],
        out_specs=pl.BlockSpec((tm, D_MODEL), lambda i: (i, 0)),
        out_shape=jax.ShapeDtypeStruct((n, D_MODEL), F32),
        scratch_shapes=[pltpu.VMEM((N_PAIRS, tm, LANES), F32)],
        compiler_params=pltpu.CompilerParams(dimension_semantics=("arbitrary",),
                                             vmem_limit_bytes=VMEM_LIMIT),
        name="mlp",
    )(x2, ypool, yattn, w_out, mlp_g, w_up, w_down)


def kernel(x, mix_norm_g, w_in, pool_w, pool_scale, q_norm_g, k_norm_g, rel_bias,
           w_out, mlp_norm_g, w_up, w_down):
    batch, seq, d = x.shape
    assert d == D_MODEL and seq % SUPER == 0 and seq % TM_PROJ == 0
    x2 = x.reshape(batch * seq, d)
    head_of = np.arange(MXU_DIM) // HEAD_DIM
    gsum = jnp.asarray((head_of[:, None] == head_of[None, :]) / HEAD_DIM, BF16)
    tile_heads = lambda g: jnp.tile(g.astype(F32), N_HEADS).reshape(1, ATTN_WIDTH)

    ypool, q, k, v = _proj_call(
        x2, mix_norm_g.reshape(1, d), w_in.astype(BF16), gsum,
        tile_heads(q_norm_g) * (HEAD_DIM ** -0.5), tile_heads(k_norm_g), pool_w.astype(BF16),
        pool_scale.reshape(1, POOL_WIDTH), seq)
    bias = _bias_call(rel_bias)
    yattn = _attn_call(q, k, v, bias, batch, seq)
    y = _mlp_call(x2, ypool, yattn, w_out.astype(BF16), mlp_norm_g.reshape(1, d),
                  w_up.astype(BF16), w_down.astype(BF16))
    return y.reshape(batch, seq, d)
```

```python
import functools
import math

import numpy as np
import jax
import jax.numpy as jnp
from jax import lax
from jax.experimental import pallas as pl
from jax.experimental.pallas import tpu as pltpu

D_MODEL = 1024
POOL_WIDTH = 512
POOL_WINDOWS = (2, 4, 8, 16)
POOL_GROUP_DIM = 128
ATTN_WIDTH = 512
HEAD_DIM = 64
N_HEADS = 8
DILATIONS = (1, 4, 16)
WINDOW_STEPS = 128
N_BUCKETS = 32
MAX_DISTANCE = 2048
D_FF = 4096
NORM_EPS = 1e-6
NEG_INF = -1e30

LANES = 128
N_PAIRS = ATTN_WIDTH // LANES
MXU_DIM = 256
MAX_DIL = 16
SUPER = WINDOW_STEPS * MAX_DIL
HALO = 32
TM_PROJ = 1024
RELAY_PITCH = TM_PROJ // MAX_DIL + 8
TM_MLP = 512
FF_CHUNK = 1024
VMEM_LIMIT = 56 * 1024 * 1024

F32 = jnp.float32
BF16 = jnp.bfloat16


def _rms(x, g):
    return x * lax.rsqrt(jnp.mean(x * x, axis=-1, keepdims=True) + NORM_EPS) * g


def _proj_kernel(x_ref, g_ref, w_ref, gsum_ref, qg_ref, kg_ref, pw_ref, ps_ref,
                 ypool_ref, q_ref, k_ref, v_ref, ubuf, sbuf, relay, *, tiles_per_seq):
    i = pl.program_id(0)
    tm = x_ref.shape[0]

    seq_tile = i % tiles_per_seq

    @pl.when(seq_tile == 0)
    def _():
        ubuf[0:HALO, :] = jnp.zeros((HALO, POOL_WIDTH), F32)

    @pl.when(seq_tile != 0)
    def _():
        ubuf[0:HALO, :] = ubuf[tm:tm + HALO, :]

    a = _rms(x_ref[...], g_ref[...]).astype(BF16)

    def head_norm(t, gain):
        sq = t * t
        head0 = lax.broadcasted_iota(jnp.int32, (tm, LANES), 1) < HEAD_DIM
        zero = jnp.zeros((tm, LANES), F32)
        ssq = []
        for c in range(0, ATTN_WIDTH, LANES):
            s = sq[:, c:c + LANES]
            s0 = jnp.sum(jnp.where(head0, s, zero), axis=-1, keepdims=True)
            s1 = jnp.sum(jnp.where(head0, zero, s), axis=-1, keepdims=True)
            ssq.append(jnp.where(head0, s0, s1))
        ms = jnp.concatenate(ssq, axis=-1) * (1.0 / HEAD_DIM)
        return t * lax.rsqrt(ms + NORM_EPS) * gain

    def put_pairs(ref, which, t):
        rows = tm // MAX_DIL
        for hp in range(N_PAIRS):
            for g in range(tm // 8):
                j, r0 = g // 2, 8 * (g % 2)
                relay[which, hp, pl.ds(r0 * RELAY_PITCH + j, 8, stride=RELAY_PITCH), :] = (
                    t[8 * g:8 * g + 8, hp * LANES:(hp + 1) * LANES])
        for hp in range(N_PAIRS):
            for r in range(MAX_DIL):
                ref[hp, :, r * LANES:(r + 1) * LANES] = (
                    relay[which, hp, r * RELAY_PITCH:r * RELAY_PITCH + rows, :].astype(BF16))

    u = jnp.dot(a, w_ref[:, :POOL_WIDTH], preferred_element_type=F32)
    ubuf[HALO:HALO + tm, :] = u
    pos = seq_tile * tm + lax.broadcasted_iota(jnp.int32, (tm, 1), 0)
    pooled = []
    end = HALO + tm
    for g, w in enumerate(POOL_WINDOWS):
        cols = slice(g * POOL_GROUP_DIM, (g + 1) * POOL_GROUP_DIM)
        levels = int(math.log2(w))
        src, span = ubuf, 1
        for lvl in range(levels):
            lo = HALO if lvl == levels - 1 else HALO - 8 * (levels - 1 - lvl)
            summed = src[lo:end, cols] + src[lo - span:end - span, cols]
            span *= 2
            if lvl == levels - 1:
                wsum = summed
            else:
                sbuf[lvl, lo:end, cols] = summed
                src = sbuf.at[lvl]
        inv_count = 1.0 / jnp.minimum(pos + 1, w).astype(F32)
        pooled.append((wsum * inv_count - ubuf[HALO:end, cols]).astype(BF16))

    q = jnp.dot(a, w_ref[:, POOL_WIDTH:POOL_WIDTH + ATTN_WIDTH], preferred_element_type=F32)
    put_pairs(q_ref, 0, head_norm(q, qg_ref[...]))
    k = jnp.dot(a, w_ref[:, POOL_WIDTH + ATTN_WIDTH:POOL_WIDTH + 2 * ATTN_WIDTH],
                preferred_element_type=F32)
    put_pairs(k_ref, 1, head_norm(k, kg_ref[...]))
    v = jnp.dot(a, w_ref[:, POOL_WIDTH + 2 * ATTN_WIDTH:], preferred_element_type=F32)
    put_pairs(v_ref, 2, v)

    for g in range(len(POOL_WINDOWS)):
        cols = slice(g * POOL_GROUP_DIM, (g + 1) * POOL_GROUP_DIM)
        mixed = jnp.dot(pooled[g], pw_ref[g], preferred_element_type=F32)
        ypool_ref[:, cols] = (mixed * ps_ref[:, cols]).astype(BF16)


def _proj_call(x2, mix_g, w_in, gsum, qg, kg, pool_w, pool_scale, seq):
    n = x2.shape[0]
    tm = TM_PROJ
    const = lambda shape: pl.BlockSpec(shape, lambda i: (0,) * len(shape))
    pair_spec = pl.BlockSpec((N_PAIRS, tm // MAX_DIL, MAX_DIL * LANES), lambda i: (0, i, 0))
    pair_shape = jax.ShapeDtypeStruct((N_PAIRS, n // MAX_DIL, MAX_DIL * LANES), BF16)
    return pl.pallas_call(
        functools.partial(_proj_kernel, tiles_per_seq=seq // tm),
        grid=(n // tm,),
        in_specs=[
            pl.BlockSpec((tm, D_MODEL), lambda i: (i, 0)),
            const((1, D_MODEL)),
            const((D_MODEL, POOL_WIDTH + 3 * ATTN_WIDTH)),
            const((MXU_DIM, MXU_DIM)),
            const((1, ATTN_WIDTH)),
            const((1, ATTN_WIDTH)),
            const((len(POOL_WINDOWS), POOL_GROUP_DIM, POOL_GROUP_DIM)),
            const((1, POOL_WIDTH)),
        ],
        out_specs=[pl.BlockSpec((tm, POOL_WIDTH), lambda i: (i, 0)), pair_spec, pair_spec, pair_spec],
        out_shape=[jax.ShapeDtypeStruct((n, POOL_WIDTH), BF16), pair_shape, pair_shape, pair_shape],
        scratch_shapes=[pltpu.VMEM((HALO + tm, POOL_WIDTH), F32),
                        pltpu.VMEM((3, HALO + tm, POOL_WIDTH), F32),
                        pltpu.VMEM((3, N_PAIRS, MAX_DIL * RELAY_PITCH, LANES), F32)],
        compiler_params=pltpu.CompilerParams(dimension_semantics=("arbitrary",),
                                             vmem_limit_bytes=VMEM_LIMIT),
        name="proj",
    )(x2, mix_g, w_in, gsum, qg, kg, pool_w, pool_scale)


def _block_orders():
    out = []
    for dil in DILATIONS:
        ns = MAX_DIL // dil
        c = WINDOW_STEPS // ns
        slab = np.arange(ns)[:, None]
        q_step = (np.arange(c)[None, :] * ns + slab + WINDOW_STEPS).reshape(-1)
        k_step = np.concatenate([q_step - WINDOW_STEPS, q_step])
        out.append((dil, ns, c, q_step, k_step))
    return out


def _bucket_tables():
    max_exact = N_BUCKETS // 2
    buckets, prev_cols = [], []
    for dil, ns, c, q_step, k_step in _block_orders():
        dist = q_step[:, None] - k_step[None, :]
        ok = (dist >= 0) & (dist <= WINDOW_STEPS)
        tok = np.clip(dist, 0, WINDOW_STEPS) * dil
        d_f = np.maximum(tok, 1).astype(np.float32)
        large = max_exact + (np.log(d_f / np.float32(max_exact)) / np.float32(math.log(MAX_DISTANCE / max_exact))
                             * np.float32(N_BUCKETS - max_exact)).astype(np.int32)
        large = np.minimum(large, N_BUCKETS - 1)
        bucket = np.where(tok < max_exact, tok, large)
        buckets.append(np.where(ok, bucket, -1).astype(np.int32))
        prev_cols.append(np.broadcast_to((k_step < WINDOW_STEPS)[None, :], dist.shape).astype(np.int32))
    return np.stack(buckets), np.stack(prev_cols)


def _bias_kernel(rel_ref, bucket_ref, prev_ref, out_ref):
    bucket = bucket_ref[0]
    is_prev = prev_ref[0] != 0
    for h in range(N_HEADS):
        tab = jnp.full(bucket.shape, NEG_INF, F32)
        for b in range(N_BUCKETS):
            tab = jnp.where(bucket == b, rel_ref[b, h], tab)
        rows = slice((h % 2) * WINDOW_STEPS, (h % 2 + 1) * WINDOW_STEPS)
        out_ref[0, 0, h // 2, rows, :] = tab
        out_ref[0, 1, h // 2, rows, :] = jnp.where(is_prev, NEG_INF, tab)


def _bias_call(rel_bias):
    bucket, prev_cols = _bucket_tables()
    npat = len(DILATIONS)
    tab_spec = pl.BlockSpec((1, WINDOW_STEPS, 2 * WINDOW_STEPS), lambda p: (p, 0, 0))
    return pl.pallas_call(
        _bias_kernel,
        grid=(npat,),
        in_specs=[pl.BlockSpec(memory_space=pltpu.SMEM), tab_spec, tab_spec],
        out_specs=pl.BlockSpec((1, 2, N_PAIRS, 2 * WINDOW_STEPS, 2 * WINDOW_STEPS),
                               lambda p: (p, 0, 0, 0, 0)),
        out_shape=jax.ShapeDtypeStruct((npat, 2, N_PAIRS, 2 * WINDOW_STEPS, 2 * WINDOW_STEPS), F32),
        name="bias",
    )(rel_bias, jnp.asarray(bucket), jnp.asarray(prev_cols))


CARRY0 = MAX_DIL


def _pattern_blocks(p):
    dil = DILATIONS[p]
    ns = MAX_DIL // dil
    c = WINDOW_STEPS // ns
    blocks = []
    for a in range(dil):
        slabs = [a + dil * j for j in range(ns)]
        for n in range(ns):
            b = a * ns + n
            blocks.append((b, slabs, c * n, c, b - 1 if n else CARRY0 + a, n == 0))
    return blocks


def _attn_kernel(q_ref, k_ref, v_ref, bias_ref, o_ref,
                 qa32, qb32, k32, v32, kc_s, vc_s, acc_s, m_s, l_s, *, supers_per_seq):
    bm = pl.program_id(1)
    is_first = bm % supers_per_seq == 0
    first = jnp.where(is_first, 1, 0)
    w = WINDOW_STEPS
    npat = len(DILATIONS)

    @pl.when(is_first)
    def _():
        for p in range(npat):
            for a in range(DILATIONS[p]):
                kc_s[p, CARRY0 + a] = jnp.zeros((w, LANES), BF16)
                vc_s[p, CARRY0 + a] = jnp.zeros((w, LANES), BF16)

    @pl.when(jnp.logical_not(is_first))
    def _():
        for p in range(npat):
            ns = MAX_DIL // DILATIONS[p]
            for a in range(DILATIONS[p]):
                kc_s[p, CARRY0 + a] = kc_s[p, a * ns + ns - 1]
                vc_s[p, CARRY0 + a] = vc_s[p, a * ns + ns - 1]

    head0 = lax.broadcasted_iota(jnp.int32, (w, LANES), 1) < HEAD_DIM
    for r in range(MAX_DIL):
        cols = slice(r * LANES, (r + 1) * LANES)
        q = q_ref[:, cols].astype(F32)
        zero = jnp.zeros_like(q)
        qa32[r] = jnp.where(head0, q, zero)
        qb32[r] = jnp.where(head0, zero, q)
        k32[r] = k_ref[:, cols].astype(F32)
        v32[r] = v_ref[:, cols].astype(F32)

    def gather(ref, slabs, row0, c):
        return jnp.concatenate([ref[s, row0:row0 + c, :] for s in slabs], axis=0)

    def attend(p, blk):
        b, slabs, row0, c, prev, head = blk
        k_cur = gather(k32, slabs, row0, c).astype(BF16)
        v_cur = gather(v32, slabs, row0, c).astype(BF16)
        qm = jnp.concatenate([gather(qa32, slabs, row0, c), gather(qb32, slabs, row0, c)], axis=0).astype(BF16)
        kk = jnp.concatenate([kc_s[p, prev], k_cur], axis=0)
        vv = jnp.concatenate([vc_s[p, prev], v_cur], axis=0)
        kc_s[p, b] = k_cur
        vc_s[p, b] = v_cur
        bias = bias_ref[p, first] if head else bias_ref[p, 0]
        s = lax.dot_general(qm, kk, (((1,), (1,)), ((), ())), preferred_element_type=F32) + bias
        m = jnp.max(s, axis=-1, keepdims=True)
        e = jnp.exp(s - m)
        l = jnp.sum(e, axis=-1, keepdims=True)
        pv = jnp.dot(e.astype(BF16), vv, preferred_element_type=F32)
        acc = jnp.where(head0, pv[:w], pv[w:])
        m_b = jnp.where(head0, m[:w], m[w:])
        l_b = jnp.where(head0, l[:w], l[w:])
        for j, sl in enumerate(slabs):
            acc_s[p, sl, row0:row0 + c, :] = acc[j * c:(j + 1) * c, :]
            m_s[p, sl, row0:row0 + c, :] = m_b[j * c:(j + 1) * c, :]
            l_s[p, sl, row0:row0 + c, :] = l_b[j * c:(j + 1) * c, :]

    for p in reversed(range(npat)):
        for blk in _pattern_blocks(p):
            attend(p, blk)

    for r in range(MAX_DIL):
        m0, m1, m2 = m_s[0, r], m_s[1, r], m_s[2, r]
        m_all = jnp.maximum(jnp.maximum(m0, m1), m2)
        w0, w1, w2 = jnp.exp(m0 - m_all), jnp.exp(m1 - m_all), jnp.exp(m2 - m_all)
        num = w0 * acc_s[0, r] + w1 * acc_s[1, r] + w2 * acc_s[2, r]
        den = w0 * l_s[0, r] + w1 * l_s[1, r] + w2 * l_s[2, r]
        o_ref[:, r * LANES:(r + 1) * LANES] = (num / den).astype(BF16)


def _attn_call(q, k, v, bias, batch, seq):
    rows, width = q.shape[1:]
    supers_per_seq = seq // SUPER
    w = WINDOW_STEPS
    tok = pl.BlockSpec((None, w, width), lambda hp, bm: (hp, bm, 0))
    npat = len(DILATIONS)
    slab = pltpu.VMEM((MAX_DIL, w, LANES), F32)
    part = pltpu.VMEM((npat, MAX_DIL, w, LANES), F32)
    chunks = pltpu.VMEM((npat, 2 * MAX_DIL, w, LANES), BF16)
    return pl.pallas_call(
        functools.partial(_attn_kernel, supers_per_seq=supers_per_seq),
        grid=(N_PAIRS, rows // w),
        in_specs=[tok, tok, tok,
                  pl.BlockSpec((npat, 2, None, 2 * w, 2 * w), lambda hp, bm: (0, 0, hp, 0, 0))],
        out_specs=tok,
        out_shape=jax.ShapeDtypeStruct((N_PAIRS, rows, width), BF16),
        scratch_shapes=[slab, slab, slab, slab, chunks, chunks, part, part, part],
        compiler_params=pltpu.CompilerParams(dimension_semantics=("arbitrary", "arbitrary"),
                                             vmem_limit_bytes=VMEM_LIMIT),
        name="attn",
    )(q, k, v, bias)


def _mlp_kernel(x_ref, ypool_ref, yattn_ref, wo_ref, g_ref, wu_ref, wd_ref, o_ref, relay):
    tm = x_ref.shape[0]
    for hp in range(N_PAIRS):
        for r in range(MAX_DIL):
            relay[hp, pl.ds(r, tm // MAX_DIL, stride=MAX_DIL), :] = (
                yattn_ref[hp, :, r * LANES:(r + 1) * LANES].astype(F32))
    mixed = jnp.concatenate([ypool_ref[...]] + [relay[hp].astype(BF16) for hp in range(N_PAIRS)], axis=-1)
    h = x_ref[...] + jnp.dot(mixed, wo_ref[...], preferred_element_type=F32)
    c = _rms(h, g_ref[...]).astype(BF16)
    acc = h
    for f in range(0, D_FF, FF_CHUNK):
        up = jnp.dot(c, wu_ref[:, f:f + FF_CHUNK], preferred_element_type=F32)
        ff = jnp.square(jnp.maximum(up, 0.0)).astype(BF16)
        acc = acc + jnp.dot(ff, wd_ref[f:f + FF_CHUNK, :], preferred_element_type=F32)
    o_ref[...] = acc


def _mlp_call(x2, ypool, yattn, w_out, mlp_g, w_up, w_down):
    n = x2.shape[0]
    tm = TM_MLP
    const = lambda shape: pl.BlockSpec(shape, lambda i: (0,) * len(shape),
                                       pipeline_mode=pl.Buffered(1))
    return pl.pallas_call(
        _mlp_kernel,
        grid=(n // tm,),
        in_specs=[
            pl.BlockSpec((tm, D_MODEL), lambda i: (i, 0)),
            pl.BlockSpec((tm, POOL_WIDTH), lambda i: (i, 0)),
            pl.BlockSpec((N_PAIRS, tm // MAX_DIL, MAX_DIL * LANES), lambda i: (0, i, 0)),
            const((D_MODEL, D_MODEL)),
            const((1, D_MODEL)),
            const((D_MODEL, D_FF)),
            const((D_FF, D_MODEL)),
        ],
        out_specs=pl.BlockSpec((tm, D_MODEL), lambda i: (i, 0)),
        out_shape=jax.ShapeDtypeStruct((n, D_MODEL), F32),
        scratch_shapes=[pltpu.VMEM((N_PAIRS, tm, LANES), F32)],
        compiler_params=pltpu.CompilerParams(dimension_semantics=("arbitrary",),
                                             vmem_limit_bytes=VMEM_LIMIT),
        name="mlp",
    )(x2, ypool, yattn, w_out, mlp_g, w_up, w_down)


def kernel(x, mix_norm_g, w_in, pool_w, pool_scale, q_norm_g, k_norm_g, rel_bias,
           w_out, mlp_norm_g, w_up, w_down):
    batch, seq, d = x.shape
    assert d == D_MODEL and seq % SUPER == 0 and seq % TM_PROJ == 0
    x2 = x.reshape(batch * seq, d)
    head_of = np.arange(MXU_DIM) // HEAD_DIM
    gsum = jnp.asarray((head_of[:, None] == head_of[None, :]) / HEAD_DIM, BF16)
    tile_heads = lambda g: jnp.tile(g.astype(F32), N_HEADS).reshape(1, ATTN_WIDTH)

    ypool, q, k, v = _proj_call(
        x2, mix_norm_g.reshape(1, d), w_in.astype(BF16), gsum,
        tile_heads(q_norm_g) * (HEAD_DIM ** -0.5), tile_heads(k_norm_g), pool_w.astype(BF16),
        pool_scale.reshape(1, POOL_WIDTH), seq)
    bias = _bias_call(rel_bias)
    yattn = _attn_call(q, k, v, bias, batch, seq)
    y = _mlp_call(x2, ypool, yattn, w_out.astype(BF16), mlp_norm_g.reshape(1, d),
                  w_up.astype(BF16), w_down.astype(BF16))
    return y.reshape(batch, seq, d)
```

```python
import functools
import math

import numpy as np
import jax
import jax.numpy as jnp
from jax import lax
from jax.experimental import pallas as pl
from jax.experimental.pallas import tpu as pltpu

D_MODEL = 1024
POOL_WIDTH = 512
POOL_WINDOWS = (2, 4, 8, 16)
POOL_GROUP_DIM = 128
ATTN_WIDTH = 512
HEAD_DIM = 64
N_HEADS = 8
DILATIONS = (1, 4, 16)
WINDOW_STEPS = 128
N_BUCKETS = 32
MAX_DISTANCE = 2048
D_FF = 4096
NORM_EPS = 1e-6
NEG_INF = -1e30

LANES = 128
N_PAIRS = ATTN_WIDTH // LANES
MAX_DIL = 16
SUPER = WINDOW_STEPS * MAX_DIL
HALO = 32
TM_PROJ = 1024
RELAY_PITCH = TM_PROJ // MAX_DIL + 8
TM_MLP = 1024
FF_CHUNK = 1024
VMEM_LIMIT = 56 * 1024 * 1024

F32 = jnp.float32
BF16 = jnp.bfloat16


def _rms(x, g):
    return x * lax.rsqrt(jnp.mean(x * x, axis=-1, keepdims=True) + NORM_EPS) * g


def _proj_kernel(x_ref, g_ref, w_ref, qg_ref, kg_ref, pw_ref, ps_ref,
                 ypool_ref, q_ref, k_ref, v_ref, ubuf, sbuf, relay, *, tiles_per_seq):
    i = pl.program_id(0)
    tm = x_ref.shape[0]

    seq_tile = i % tiles_per_seq

    @pl.when(seq_tile == 0)
    def _():
        ubuf[0:HALO, :] = jnp.zeros((HALO, POOL_WIDTH), F32)

    @pl.when(seq_tile != 0)
    def _():
        ubuf[0:HALO, :] = ubuf[tm:tm + HALO, :]

    a = _rms(x_ref[...], g_ref[...]).astype(BF16)

    def head_norm(t, gain):
        sq = t * t
        head0 = lax.broadcasted_iota(jnp.int32, (tm, LANES), 1) < HEAD_DIM
        zero = jnp.zeros((tm, LANES), F32)
        ssq = []
        for c in range(0, ATTN_WIDTH, LANES):
            s = sq[:, c:c + LANES]
            s0 = jnp.sum(jnp.where(head0, s, zero), axis=-1, keepdims=True)
            s1 = jnp.sum(jnp.where(head0, zero, s), axis=-1, keepdims=True)
            ssq.append(jnp.where(head0, s0, s1))
        ms = jnp.concatenate(ssq, axis=-1) * (1.0 / HEAD_DIM)
        return t * lax.rsqrt(ms + NORM_EPS) * gain

    def put_pairs(ref, which, t):
        rows = tm // MAX_DIL
        for hp in range(N_PAIRS):
            for g in range(tm // 8):
                j, r0 = g // 2, 8 * (g % 2)
                relay[which, hp, pl.ds(r0 * RELAY_PITCH + j, 8, stride=RELAY_PITCH), :] = (
                    t[8 * g:8 * g + 8, hp * LANES:(hp + 1) * LANES])
        for hp in range(N_PAIRS):
            for r in range(MAX_DIL):
                ref[hp, :, r * LANES:(r + 1) * LANES] = (
                    relay[which, hp, r * RELAY_PITCH:r * RELAY_PITCH + rows, :].astype(BF16))

    u = jnp.dot(a, w_ref[:, :POOL_WIDTH], preferred_element_type=F32)
    ubuf[HALO:HALO + tm, :] = u
    pos = seq_tile * tm + lax.broadcasted_iota(jnp.int32, (tm, 1), 0)
    pooled = []
    end = HALO + tm
    for g, w in enumerate(POOL_WINDOWS):
        cols = slice(g * POOL_GROUP_DIM, (g + 1) * POOL_GROUP_DIM)
        levels = int(math.log2(w))
        src, span = ubuf, 1
        for lvl in range(levels):
            lo = HALO if lvl == levels - 1 else HALO - 8 * (levels - 1 - lvl)
            summed = src[lo:end, cols] + src[lo - span:end - span, cols]
            span *= 2
            if lvl == levels - 1:
                wsum = summed
            else:
                sbuf[lvl, lo:end, cols] = summed
                src = sbuf.at[lvl]
        inv_count = 1.0 / jnp.minimum(pos + 1, w).astype(F32)
        pooled.append((wsum * inv_count - ubuf[HALO:end, cols]).astype(BF16))

    q = jnp.dot(a, w_ref[:, POOL_WIDTH:POOL_WIDTH + ATTN_WIDTH], preferred_element_type=F32)
    put_pairs(q_ref, 0, head_norm(q, qg_ref[...]))
    k = jnp.dot(a, w_ref[:, POOL_WIDTH + ATTN_WIDTH:POOL_WIDTH + 2 * ATTN_WIDTH],
                preferred_element_type=F32)
    put_pairs(k_ref, 1, head_norm(k, kg_ref[...]))
    v = jnp.dot(a, w_ref[:, POOL_WIDTH + 2 * ATTN_WIDTH:], preferred_element_type=F32)
    put_pairs(v_ref, 2, v)

    for g in range(len(POOL_WINDOWS)):
        cols = slice(g * POOL_GROUP_DIM, (g + 1) * POOL_GROUP_DIM)
        mixed = jnp.dot(pooled[g], pw_ref[g], preferred_element_type=F32)
        ypool_ref[:, cols] = (mixed * ps_ref[:, cols]).astype(BF16)


def _proj_call(x2, mix_g, w_in, qg, kg, pool_w, pool_scale, seq):
    n = x2.shape[0]
    tm = TM_PROJ
    const = lambda shape: pl.BlockSpec(shape, lambda i: (0,) * len(shape))
    pair_spec = pl.BlockSpec((N_PAIRS, tm // MAX_DIL, MAX_DIL * LANES), lambda i: (0, i, 0))
    pair_shape = jax.ShapeDtypeStruct((N_PAIRS, n // MAX_DIL, MAX_DIL * LANES), BF16)
    return pl.pallas_call(
        functools.partial(_proj_kernel, tiles_per_seq=seq // tm),
        grid=(n // tm,),
        in_specs=[
            pl.BlockSpec((tm, D_MODEL), lambda i: (i, 0)),
            const((1, D_MODEL)),
            const((D_MODEL, POOL_WIDTH + 3 * ATTN_WIDTH)),
            const((1, ATTN_WIDTH)),
            const((1, ATTN_WIDTH)),
            const((len(POOL_WINDOWS), POOL_GROUP_DIM, POOL_GROUP_DIM)),
            const((1, POOL_WIDTH)),
        ],
        out_specs=[pl.BlockSpec((tm, POOL_WIDTH), lambda i: (i, 0)), pair_spec, pair_spec, pair_spec],
        out_shape=[jax.ShapeDtypeStruct((n, POOL_WIDTH), BF16), pair_shape, pair_shape, pair_shape],
        scratch_shapes=[pltpu.VMEM((HALO + tm, POOL_WIDTH), F32),
                        pltpu.VMEM((3, HALO + tm, POOL_WIDTH), F32),
                        pltpu.VMEM((3, N_PAIRS, MAX_DIL * RELAY_PITCH, LANES), F32)],
        compiler_params=pltpu.CompilerParams(dimension_semantics=("arbitrary",),
                                             vmem_limit_bytes=VMEM_LIMIT),
        name="proj",
    )(x2, mix_g, w_in, qg, kg, pool_w, pool_scale)


def _block_orders():
    out = []
    for dil in DILATIONS:
        ns = MAX_DIL // dil
        c = WINDOW_STEPS // ns
        slab = np.arange(ns)[:, None]
        q_step = (np.arange(c)[None, :] * ns + slab + WINDOW_STEPS).reshape(-1)
        k_step = np.concatenate([q_step - WINDOW_STEPS, q_step])
        out.append((dil, ns, c, q_step, k_step))
    return out


def _bucket_tables():
    max_exact = N_BUCKETS // 2
    buckets, prev_cols = [], []
    for dil, ns, c, q_step, k_step in _block_orders():
        dist = q_step[:, None] - k_step[None, :]
        ok = (dist >= 0) & (dist <= WINDOW_STEPS)
        tok = np.clip(dist, 0, WINDOW_STEPS) * dil
        d_f = np.maximum(tok, 1).astype(np.float32)
        large = max_exact + (np.log(d_f / np.float32(max_exact)) / np.float32(math.log(MAX_DISTANCE / max_exact))
                             * np.float32(N_BUCKETS - max_exact)).astype(np.int32)
        large = np.minimum(large, N_BUCKETS - 1)
        bucket = np.where(tok < max_exact, tok, large)
        buckets.append(np.where(ok, bucket, -1).astype(np.int32))
        prev_cols.append(np.broadcast_to((k_step < WINDOW_STEPS)[None, :], dist.shape).astype(np.int32))
    return np.stack(buckets), np.stack(prev_cols)


def _bias_kernel(rel_ref, bucket_ref, prev_ref, out_ref):
    bucket = bucket_ref[0]
    is_prev = prev_ref[0] != 0
    for h in range(N_HEADS):
        tab = jnp.full(bucket.shape, NEG_INF, F32)
        for b in range(N_BUCKETS):
            tab = jnp.where(bucket == b, rel_ref[b, h], tab)
        rows = slice((h % 2) * WINDOW_STEPS, (h % 2 + 1) * WINDOW_STEPS)
        out_ref[0, 0, h // 2, rows, :] = tab
        out_ref[0, 1, h // 2, rows, :] = jnp.where(is_prev, NEG_INF, tab)


def _bias_call(rel_bias):
    bucket, prev_cols = _bucket_tables()
    npat = len(DILATIONS)
    tab_spec = pl.BlockSpec((1, WINDOW_STEPS, 2 * WINDOW_STEPS), lambda p: (p, 0, 0))
    return pl.pallas_call(
        _bias_kernel,
        grid=(npat,),
        in_specs=[pl.BlockSpec(memory_space=pltpu.SMEM), tab_spec, tab_spec],
        out_specs=pl.BlockSpec((1, 2, N_PAIRS, 2 * WINDOW_STEPS, 2 * WINDOW_STEPS),
                               lambda p: (p, 0, 0, 0, 0)),
        out_shape=jax.ShapeDtypeStruct((npat, 2, N_PAIRS, 2 * WINDOW_STEPS, 2 * WINDOW_STEPS), F32),
        name="bias",
    )(rel_bias, jnp.asarray(bucket), jnp.asarray(prev_cols))


CARRY0 = MAX_DIL


def _pattern_blocks(p):
    dil = DILATIONS[p]
    ns = MAX_DIL // dil
    c = WINDOW_STEPS // ns
    blocks = []
    for a in range(dil):
        slabs = [a + dil * j for j in range(ns)]
        for n in range(ns):
            b = a * ns + n
            blocks.append((b, slabs, c * n, c, b - 1 if n else CARRY0 + a, n == 0))
    return blocks


def _attn_kernel(q_ref, k_ref, v_ref, bias_ref, o_ref,
                 qa32, qb32, k32, v32, kc_s, vc_s, acc_s, m_s, l_s, *, supers_per_seq):
    bm = pl.program_id(1)
    is_first = bm % supers_per_seq == 0
    first = jnp.where(is_first, 1, 0)
    w = WINDOW_STEPS
    npat = len(DILATIONS)

    @pl.when(is_first)
    def _():
        for p in range(npat):
            for a in range(DILATIONS[p]):
                kc_s[p, CARRY0 + a] = jnp.zeros((w, LANES), BF16)
                vc_s[p, CARRY0 + a] = jnp.zeros((w, LANES), BF16)

    @pl.when(jnp.logical_not(is_first))
    def _():
        for p in range(npat):
            ns = MAX_DIL // DILATIONS[p]
            for a in range(DILATIONS[p]):
                kc_s[p, CARRY0 + a] = kc_s[p, a * ns + ns - 1]
                vc_s[p, CARRY0 + a] = vc_s[p, a * ns + ns - 1]

    head0 = lax.broadcasted_iota(jnp.int32, (w, LANES), 1) < HEAD_DIM
    for r in range(MAX_DIL):
        cols = slice(r * LANES, (r + 1) * LANES)
        q = q_ref[:, cols].astype(F32)
        zero = jnp.zeros_like(q)
        qa32[r] = jnp.where(head0, q, zero)
        qb32[r] = jnp.where(head0, zero, q)
        k32[r] = k_ref[:, cols].astype(F32)
        v32[r] = v_ref[:, cols].astype(F32)

    def gather(ref, slabs, row0, c):
        return jnp.concatenate([ref[s, row0:row0 + c, :] for s in slabs], axis=0)

    def attend(p, blk):
        b, slabs, row0, c, prev, head = blk
        k_cur = gather(k32, slabs, row0, c).astype(BF16)
        v_cur = gather(v32, slabs, row0, c).astype(BF16)
        qm = jnp.concatenate([gather(qa32, slabs, row0, c), gather(qb32, slabs, row0, c)], axis=0).astype(BF16)
        kk = jnp.concatenate([kc_s[p, prev], k_cur], axis=0)
        vv = jnp.concatenate([vc_s[p, prev], v_cur], axis=0)
        kc_s[p, b] = k_cur
        vc_s[p, b] = v_cur
        bias = bias_ref[p, first] if head else bias_ref[p, 0]
        s = lax.dot_general(qm, kk, (((1,), (1,)), ((), ())), preferred_element_type=F32) + bias
        m = jnp.max(s, axis=-1, keepdims=True)
        e = jnp.exp(s - m)
        l = jnp.sum(e, axis=-1, keepdims=True)
        pv = jnp.dot(e.astype(BF16), vv, preferred_element_type=F32)
        acc = jnp.where(head0, pv[:w], pv[w:])
        m_b = jnp.where(head0, m[:w], m[w:])
        l_b = jnp.where(head0, l[:w], l[w:])
        for j, sl in enumerate(slabs):
            acc_s[p, sl, row0:row0 + c, :] = acc[j * c:(j + 1) * c, :]
            m_s[p, sl, row0:row0 + c, :] = m_b[j * c:(j + 1) * c, :]
            l_s[p, sl, row0:row0 + c, :] = l_b[j * c:(j + 1) * c, :]

    for p in reversed(range(npat)):
        for blk in _pattern_blocks(p):
            attend(p, blk)

    for r in range(MAX_DIL):
        m0, m1, m2 = m_s[0, r], m_s[1, r], m_s[2, r]
        m_all = jnp.maximum(jnp.maximum(m0, m1), m2)
        w0, w1, w2 = jnp.exp(m0 - m_all), jnp.exp(m1 - m_all), jnp.exp(m2 - m_all)
        num = w0 * acc_s[0, r] + w1 * acc_s[1, r] + w2 * acc_s[2, r]
        den = w0 * l_s[0, r] + w1 * l_s[1, r] + w2 * l_s[2, r]
        o_ref[:, r * LANES:(r + 1) * LANES] = (num / den).astype(BF16)


def _attn_call(q, k, v, bias, batch, seq):
    rows, width = q.shape[1:]
    supers_per_seq = seq // SUPER
    w = WINDOW_STEPS
    tok = pl.BlockSpec((None, w, width), lambda hp, bm: (hp, bm, 0))
    npat = len(DILATIONS)
    slab = pltpu.VMEM((MAX_DIL, w, LANES), F32)
    part = pltpu.VMEM((npat, MAX_DIL, w, LANES), F32)
    chunks = pltpu.VMEM((npat, 2 * MAX_DIL, w, LANES), BF16)
    return pl.pallas_call(
        functools.partial(_attn_kernel, supers_per_seq=supers_per_seq),
        grid=(N_PAIRS, rows // w),
        in_specs=[tok, tok, tok,
                  pl.BlockSpec((npat, 2, None, 2 * w, 2 * w), lambda hp, bm: (0, 0, hp, 0, 0))],
        out_specs=tok,
        out_shape=jax.ShapeDtypeStruct((N_PAIRS, rows, width), BF16),
        scratch_shapes=[slab, slab, slab, slab, chunks, chunks, part, part, part],
        compiler_params=pltpu.CompilerParams(dimension_semantics=("arbitrary", "arbitrary"),
                                             vmem_limit_bytes=VMEM_LIMIT),
        name="attn",
    )(q, k, v, bias)


def _mlp_kernel(x_ref, ypool_ref, yattn_ref, wo_ref, g_ref, wu_ref, wd_ref, o_ref, relay):
    tm = x_ref.shape[0]
    for hp in range(N_PAIRS):
        for r in range(MAX_DIL):
            relay[hp, pl.ds(r, tm // MAX_DIL, stride=MAX_DIL), :] = (
                yattn_ref[hp, :, r * LANES:(r + 1) * LANES].astype(F32))
    mixed = jnp.concatenate([ypool_ref[...]] + [relay[hp].astype(BF16) for hp in range(N_PAIRS)], axis=-1)
    h = x_ref[...] + jnp.dot(mixed, wo_ref[...], preferred_element_type=F32)
    c = _rms(h, g_ref[...]).astype(BF16)
    acc = h
    for f in range(0, D_FF, FF_CHUNK):
        up = jnp.dot(c, wu_ref[:, f:f + FF_CHUNK], preferred_element_type=F32)
        ff = jnp.square(jnp.maximum(up, 0.0)).astype(BF16)
        acc = acc + jnp.dot(ff, wd_ref[f:f + FF_CHUNK, :], preferred_element_type=F32)
    o_ref[...] = acc


def _mlp_call(x2, ypool, yattn, w_out, mlp_g, w_up, w_down):
    n = x2.shape[0]
    tm = TM_MLP
    const = lambda shape: pl.BlockSpec(shape, lambda i: (0,) * len(shape),
                                       pipeline_mode=pl.Buffered(1))
    return pl.pallas_call(
        _mlp_kernel,
        grid=(n // tm,),
        in_specs=[
            pl.BlockSpec((tm, D_MODEL), lambda i: (i, 0)),
            pl.BlockSpec((tm, POOL_WIDTH), lambda i: (i, 0)),
            pl.BlockSpec((N_PAIRS, tm // MAX_DIL, MAX_DIL * LANES), lambda i: (0, i, 0)),
            const((D_MODEL, D_MODEL)),
            const((1, D_MODEL)),
            const((D_MODEL, D_FF)),
            const((D_FF, D_MODEL)),
        ],
        out_specs=pl.BlockSpec((tm, D_MODEL), lambda i: (i, 0)),
        out_shape=jax.ShapeDtypeStruct((n, D_MODEL), F32),
        scratch_shapes=[pltpu.VMEM((N_PAIRS, tm, LANES), F32)],
        compiler_params=pltpu.CompilerParams(dimension_semantics=("arbitrary",),
                                             vmem_limit_bytes=VMEM_LIMIT),
        name="mlp",
    )(x2, ypool, yattn, w_out, mlp_g, w_up, w_down)


def kernel(x, mix_norm_g, w_in, pool_w, pool_scale, q_norm_g, k_norm_g, rel_bias,
           w_out, mlp_norm_g, w_up, w_down):
    batch, seq, d = x.shape
    assert d == D_MODEL and seq % SUPER == 0 and seq % TM_PROJ == 0
    x2 = x.reshape(batch * seq, d)
    tile_heads = lambda g: jnp.tile(g.astype(F32), N_HEADS).reshape(1, ATTN_WIDTH)

    ypool, q, k, v = _proj_call(
        x2, mix_norm_g.reshape(1, d), w_in.astype(BF16),
        tile_heads(q_norm_g) * (HEAD_DIM ** -0.5), tile_heads(k_norm_g), pool_w.astype(BF16),
        pool_scale.reshape(1, POOL_WIDTH), seq)
    bias = _bias_call(rel_bias)
    yattn = _attn_call(q, k, v, bias, batch, seq)
    y = _mlp_call(x2, ypool, yattn, w_out.astype(BF16), mlp_norm_g.reshape(1, d),
                  w_up.astype(BF16), w_down.astype(BF16))
    return y.reshape(batch, seq, d)
```

```python
import functools
import math

import numpy as np
import jax
import jax.numpy as jnp
from jax import lax
from jax.experimental import pallas as pl
from jax.experimental.pallas import tpu as pltpu

D_MODEL = 1024
POOL_WIDTH = 512
POOL_WINDOWS = (2, 4, 8, 16)
POOL_GROUP_DIM = 128
ATTN_WIDTH = 512
HEAD_DIM = 64
N_HEADS = 8
DILATIONS = (1, 4, 16)
WINDOW_STEPS = 128
N_BUCKETS = 32
MAX_DISTANCE = 2048
D_FF = 4096
NORM_EPS = 1e-6
NEG_INF = -1e30

LANES = 128
N_PAIRS = ATTN_WIDTH // LANES
MAX_DIL = 16
SUPER = WINDOW_STEPS * MAX_DIL
HALO = 32
TM_PROJ = 1024
RELAY_PITCH = TM_PROJ // MAX_DIL + 8
TM_MLP = 1024
FF_CHUNK = 1024
VMEM_LIMIT = 56 * 1024 * 1024

F32 = jnp.float32
BF16 = jnp.bfloat16


def _rms(x, g):
    return x * lax.rsqrt(jnp.mean(x * x, axis=-1, keepdims=True) + NORM_EPS) * g


def _proj_kernel(x_ref, g_ref, w_ref, qg_ref, kg_ref, pw_ref, ps_ref,
                 ypool_ref, q_ref, k_ref, v_ref, ubuf, sbuf, relay, *, tiles_per_seq):
    i = pl.program_id(0)
    tm = x_ref.shape[0]

    seq_tile = i % tiles_per_seq

    @pl.when(seq_tile == 0)
    def _():
        ubuf[0:HALO, :] = jnp.zeros((HALO, POOL_WIDTH), F32)

    @pl.when(seq_tile != 0)
    def _():
        ubuf[0:HALO, :] = ubuf[tm:tm + HALO, :]

    a = _rms(x_ref[...], g_ref[...]).astype(BF16)

    def head_norm(t, gain):
        sq = t * t
        head0 = lax.broadcasted_iota(jnp.int32, (tm, LANES), 1) < HEAD_DIM
        zero = jnp.zeros((tm, LANES), F32)
        ssq = []
        for c in range(0, ATTN_WIDTH, LANES):
            s = sq[:, c:c + LANES]
            s0 = jnp.sum(jnp.where(head0, s, zero), axis=-1, keepdims=True)
            s1 = jnp.sum(jnp.where(head0, zero, s), axis=-1, keepdims=True)
            ssq.append(jnp.where(head0, s0, s1))
        ms = jnp.concatenate(ssq, axis=-1) * (1.0 / HEAD_DIM)
        return t * lax.rsqrt(ms + NORM_EPS) * gain

    def put_pairs(ref, which, t):
        rows = tm // MAX_DIL
        for hp in range(N_PAIRS):
            for g in range(tm // 8):
                j, r0 = g // 2, 8 * (g % 2)
                relay[which, hp, pl.ds(r0 * RELAY_PITCH + j, 8, stride=RELAY_PITCH), :] = (
                    t[8 * g:8 * g + 8, hp * LANES:(hp + 1) * LANES])
        for hp in range(N_PAIRS):
            for r in range(MAX_DIL):
                ref[hp, :, r * LANES:(r + 1) * LANES] = (
                    relay[which, hp, r * RELAY_PITCH:r * RELAY_PITCH + rows, :].astype(BF16))

    u = jnp.dot(a, w_ref[:, :POOL_WIDTH], preferred_element_type=F32)
    ubuf[HALO:HALO + tm, :] = u
    pos = seq_tile * tm + lax.broadcasted_iota(jnp.int32, (tm, 1), 0)
    pooled = []
    end = HALO + tm
    for g, w in enumerate(POOL_WINDOWS):
        cols = slice(g * POOL_GROUP_DIM, (g + 1) * POOL_GROUP_DIM)
        levels = int(math.log2(w))
        src, span = ubuf, 1
        for lvl in range(levels):
            lo = HALO if lvl == levels - 1 else HALO - 8 * (levels - 1 - lvl)
            summed = src[lo:end, cols] + src[lo - span:end - span, cols]
            span *= 2
            if lvl == levels - 1:
                wsum = summed
            else:
                sbuf[lvl, lo:end, cols] = summed
                src = sbuf.at[lvl]
        inv_count = 1.0 / jnp.minimum(pos + 1, w).astype(F32)
        pooled.append((wsum * inv_count - ubuf[HALO:end, cols]).astype(BF16))

    q = jnp.dot(a, w_ref[:, POOL_WIDTH:POOL_WIDTH + ATTN_WIDTH], preferred_element_type=F32)
    put_pairs(q_ref, 0, head_norm(q, qg_ref[...]))
    k = jnp.dot(a, w_ref[:, POOL_WIDTH + ATTN_WIDTH:POOL_WIDTH + 2 * ATTN_WIDTH],
                preferred_element_type=F32)
    put_pairs(k_ref, 1, head_norm(k, kg_ref[...]))
    v = jnp.dot(a, w_ref[:, POOL_WIDTH + 2 * ATTN_WIDTH:], preferred_element_type=F32)
    put_pairs(v_ref, 2, v)

    for g in range(len(POOL_WINDOWS)):
        cols = slice(g * POOL_GROUP_DIM, (g + 1) * POOL_GROUP_DIM)
        mixed = jnp.dot(pooled[g], pw_ref[g], preferred_element_type=F32)
        ypool_ref[:, cols] = (mixed * ps_ref[:, cols]).astype(BF16)


def _proj_call(x2, mix_g, w_in, qg, kg, pool_w, pool_scale, seq):
    n = x2.shape[0]
    tm = TM_PROJ
    const = lambda shape: pl.BlockSpec(shape, lambda i: (0,) * len(shape))
    pair_spec = pl.BlockSpec((N_PAIRS, tm // MAX_DIL, MAX_DIL * LANES), lambda i: (0, i, 0))
    pair_shape = jax.ShapeDtypeStruct((N_PAIRS, n // MAX_DIL, MAX_DIL * LANES), BF16)
    return pl.pallas_call(
        functools.partial(_proj_kernel, tiles_per_seq=seq // tm),
        grid=(n // tm,),
        in_specs=[
            pl.BlockSpec((tm, D_MODEL), lambda i: (i, 0)),
            const((1, D_MODEL)),
            const((D_MODEL, POOL_WIDTH + 3 * ATTN_WIDTH)),
            const((1, ATTN_WIDTH)),
            const((1, ATTN_WIDTH)),
            const((len(POOL_WINDOWS), POOL_GROUP_DIM, POOL_GROUP_DIM)),
            const((1, POOL_WIDTH)),
        ],
        out_specs=[pl.BlockSpec((tm, POOL_WIDTH), lambda i: (i, 0)), pair_spec, pair_spec, pair_spec],
        out_shape=[jax.ShapeDtypeStruct((n, POOL_WIDTH), BF16), pair_shape, pair_shape, pair_shape],
        scratch_shapes=[pltpu.VMEM((HALO + tm, POOL_WIDTH), F32),
                        pltpu.VMEM((3, HALO + tm, POOL_WIDTH), F32),
                        pltpu.VMEM((3, N_PAIRS, MAX_DIL * RELAY_PITCH, LANES), F32)],
        compiler_params=pltpu.CompilerParams(dimension_semantics=("arbitrary",),
                                             vmem_limit_bytes=VMEM_LIMIT),
        name="proj",
    )(x2, mix_g, w_in, qg, kg, pool_w, pool_scale)


def _block_orders():
    out = []
    for dil in DILATIONS:
        ns = MAX_DIL // dil
        c = WINDOW_STEPS // ns
        slab = np.arange(ns)[:, None]
        q_step = (np.arange(c)[None, :] * ns + slab + WINDOW_STEPS).reshape(-1)
        k_step = np.concatenate([q_step - WINDOW_STEPS, q_step])
        out.append((dil, ns, c, q_step, k_step))
    return out


def _bucket_tables():
    max_exact = N_BUCKETS // 2
    buckets, prev_cols = [], []
    for dil, ns, c, q_step, k_step in _block_orders():
        dist = q_step[:, None] - k_step[None, :]
        ok = (dist >= 0) & (dist <= WINDOW_STEPS)
        tok = np.clip(dist, 0, WINDOW_STEPS) * dil
        d_f = np.maximum(tok, 1).astype(np.float32)
        large = max_exact + (np.log(d_f / np.float32(max_exact)) / np.float32(math.log(MAX_DISTANCE / max_exact))
                             * np.float32(N_BUCKETS - max_exact)).astype(np.int32)
        large = np.minimum(large, N_BUCKETS - 1)
        bucket = np.where(tok < max_exact, tok, large)
        buckets.append(np.where(ok, bucket, -1).astype(np.int32))
        prev_cols.append(np.broadcast_to((k_step < WINDOW_STEPS)[None, :], dist.shape).astype(np.int32))
    return np.stack(buckets), np.stack(prev_cols)


def _bias_kernel(rel_ref, bucket_ref, prev_ref, out_ref):
    bucket = bucket_ref[0]
    is_prev = prev_ref[0] != 0
    for h in range(N_HEADS):
        tab = jnp.full(bucket.shape, NEG_INF, F32)
        for b in range(N_BUCKETS):
            tab = jnp.where(bucket == b, rel_ref[b, h], tab)
        rows = slice((h % 2) * WINDOW_STEPS, (h % 2 + 1) * WINDOW_STEPS)
        out_ref[0, 0, h // 2, rows, :] = tab
        out_ref[0, 1, h // 2, rows, :] = jnp.where(is_prev, NEG_INF, tab)


def _bias_call(rel_bias):
    bucket, prev_cols = _bucket_tables()
    npat = len(DILATIONS)
    tab_spec = pl.BlockSpec((1, WINDOW_STEPS, 2 * WINDOW_STEPS), lambda p: (p, 0, 0))
    return pl.pallas_call(
        _bias_kernel,
        grid=(npat,),
        in_specs=[pl.BlockSpec(memory_space=pltpu.SMEM), tab_spec, tab_spec],
        out_specs=pl.BlockSpec((1, 2, N_PAIRS, 2 * WINDOW_STEPS, 2 * WINDOW_STEPS),
                               lambda p: (p, 0, 0, 0, 0)),
        out_shape=jax.ShapeDtypeStruct((npat, 2, N_PAIRS, 2 * WINDOW_STEPS, 2 * WINDOW_STEPS), F32),
        name="bias",
    )(rel_bias, jnp.asarray(bucket), jnp.asarray(prev_cols))


TILES = 2


def _pattern_blocks(p):
    dil = DILATIONS[p]
    ns = MAX_DIL // dil
    c = WINDOW_STEPS // ns
    return [(a * ns + n, [a + dil * j for j in range(ns)], c * n, c, a, n)
            for a in range(dil) for n in range(ns)]


def _attn_kernel(q_ref, k_ref, v_ref, bias_ref, o_ref,
                 qa32, qb32, k32, v32, kc_s, vc_s, kcarry, vcarry, acc_s, m_s, l_s, *, supers_per_seq):
    step = pl.program_id(1)
    is_first = (step * TILES) % supers_per_seq == 0
    first = jnp.where(is_first, 1, 0)
    w = WINDOW_STEPS
    npat = len(DILATIONS)

    @pl.when(is_first)
    def _():
        for p in range(npat):
            for a in range(DILATIONS[p]):
                kcarry[p, a] = jnp.zeros((w, LANES), BF16)
                vcarry[p, a] = jnp.zeros((w, LANES), BF16)

    @pl.when(jnp.logical_not(is_first))
    def _():
        for p in range(npat):
            ns = MAX_DIL // DILATIONS[p]
            for a in range(DILATIONS[p]):
                kcarry[p, a] = kc_s[TILES - 1, p, a * ns + ns - 1]
                vcarry[p, a] = vc_s[TILES - 1, p, a * ns + ns - 1]

    head0 = lax.broadcasted_iota(jnp.int32, (w, LANES), 1) < HEAD_DIM

    def load_slabs(t):
        rows = slice(t * w, (t + 1) * w)
        for r in range(MAX_DIL):
            cols = slice(r * LANES, (r + 1) * LANES)
            q = q_ref[rows, cols].astype(F32)
            zero = jnp.zeros_like(q)
            qa32[t, r] = jnp.where(head0, q, zero)
            qb32[t, r] = jnp.where(head0, zero, q)
            k32[t, r] = k_ref[rows, cols].astype(F32)
            v32[t, r] = v_ref[rows, cols].astype(F32)

    def gather(ref, t, slabs, row0, c):
        return jnp.concatenate([ref[t, s, row0:row0 + c, :] for s in slabs], axis=0)

    def attend(t, p, blk):
        b, slabs, row0, c, stream, n = blk
        ns = MAX_DIL // DILATIONS[p]
        k_cur = gather(k32, t, slabs, row0, c).astype(BF16)
        v_cur = gather(v32, t, slabs, row0, c).astype(BF16)
        qm = jnp.concatenate([gather(qa32, t, slabs, row0, c), gather(qb32, t, slabs, row0, c)],
                             axis=0).astype(BF16)
        if n:
            k_prev, v_prev = kc_s[t, p, b - 1], vc_s[t, p, b - 1]
        elif t:
            k_prev, v_prev = kc_s[t - 1, p, b + ns - 1], vc_s[t - 1, p, b + ns - 1]
        else:
            k_prev, v_prev = kcarry[p, stream], vcarry[p, stream]
        kk = jnp.concatenate([k_prev, k_cur], axis=0)
        vv = jnp.concatenate([v_prev, v_cur], axis=0)
        kc_s[t, p, b] = k_cur
        vc_s[t, p, b] = v_cur
        bias = bias_ref[p, first] if (n == 0 and t == 0) else bias_ref[p, 0]
        s = lax.dot_general(qm, kk, (((1,), (1,)), ((), ())), preferred_element_type=F32) + bias
        m = jnp.max(s, axis=-1, keepdims=True)
        e = jnp.exp(s - m)
        l = jnp.sum(e, axis=-1, keepdims=True)
        pv = jnp.dot(e.astype(BF16), vv, preferred_element_type=F32)
        acc = jnp.where(head0, pv[:w], pv[w:])
        m_b = jnp.where(head0, m[:w], m[w:])
        l_b = jnp.where(head0, l[:w], l[w:])
        for j, sl in enumerate(slabs):
            acc_s[t, p, sl, row0:row0 + c, :] = acc[j * c:(j + 1) * c, :]
            m_s[t, p, sl, row0:row0 + c, :] = m_b[j * c:(j + 1) * c, :]
            l_s[t, p, sl, row0:row0 + c, :] = l_b[j * c:(j + 1) * c, :]

    def combine(t, r):
        m0, m1, m2 = m_s[t, 0, r], m_s[t, 1, r], m_s[t, 2, r]
        m_all = jnp.maximum(jnp.maximum(m0, m1), m2)
        w0, w1, w2 = jnp.exp(m0 - m_all), jnp.exp(m1 - m_all), jnp.exp(m2 - m_all)
        num = w0 * acc_s[t, 0, r] + w1 * acc_s[t, 1, r] + w2 * acc_s[t, 2, r]
        den = w0 * l_s[t, 0, r] + w1 * l_s[t, 1, r] + w2 * l_s[t, 2, r]
        o_ref[t * w:(t + 1) * w, r * LANES:(r + 1) * LANES] = (num / den).astype(BF16)

    for t in range(TILES):
        load_slabs(t)
    for t in range(TILES):
        for p in reversed(range(npat)):
            for blk in _pattern_blocks(p):
                attend(t, p, blk)
        for r in range(MAX_DIL):
            combine(t, r)


def _attn_call(q, k, v, bias, batch, seq):
    rows, width = q.shape[1:]
    supers_per_seq = seq // SUPER
    assert supers_per_seq % TILES == 0
    w = WINDOW_STEPS
    tok = pl.BlockSpec((None, TILES * w, width), lambda hp, st: (hp, st, 0))
    npat = len(DILATIONS)
    slab = pltpu.VMEM((TILES, MAX_DIL, w, LANES), F32)
    part = pltpu.VMEM((TILES, npat, MAX_DIL, w, LANES), F32)
    chunks = pltpu.VMEM((TILES, npat, MAX_DIL, w, LANES), BF16)
    carry = pltpu.VMEM((npat, MAX_DIL, w, LANES), BF16)
    return pl.pallas_call(
        functools.partial(_attn_kernel, supers_per_seq=supers_per_seq),
        grid=(N_PAIRS, rows // (TILES * w)),
        in_specs=[tok, tok, tok,
                  pl.BlockSpec((npat, 2, None, 2 * w, 2 * w), lambda hp, st: (0, 0, hp, 0, 0))],
        out_specs=tok,
        out_shape=jax.ShapeDtypeStruct((N_PAIRS, rows, width), BF16),
        scratch_shapes=[slab, slab, slab, slab, chunks, chunks, carry, carry, part, part, part],
        compiler_params=pltpu.CompilerParams(dimension_semantics=("arbitrary", "arbitrary"),
                                             vmem_limit_bytes=VMEM_LIMIT),
        name="attn",
    )(q, k, v, bias)


def _mlp_kernel(x_ref, ypool_ref, yattn_ref, wo_ref, g_ref, wu_ref, wd_ref, o_ref, relay):
    tm = x_ref.shape[0]
    for hp in range(N_PAIRS):
        for r in range(MAX_DIL):
            relay[hp, pl.ds(r, tm // MAX_DIL, stride=MAX_DIL), :] = (
                yattn_ref[hp, :, r * LANES:(r + 1) * LANES].astype(F32))
    mixed = jnp.concatenate([ypool_ref[...]] + [relay[hp].astype(BF16) for hp in range(N_PAIRS)], axis=-1)
    h = x_ref[...] + jnp.dot(mixed, wo_ref[...], preferred_element_type=F32)
    c = _rms(h, g_ref[...]).astype(BF16)
    acc = h
    for f in range(0, D_FF, FF_CHUNK):
        up = jnp.dot(c, wu_ref[:, f:f + FF_CHUNK], preferred_element_type=F32)
        ff = jnp.square(jnp.maximum(up, 0.0)).astype(BF16)
        acc = acc + jnp.dot(ff, wd_ref[f:f + FF_CHUNK, :], preferred_element_type=F32)
    o_ref[...] = acc


def _mlp_call(x2, ypool, yattn, w_out, mlp_g, w_up, w_down):
    n = x2.shape[0]
    tm = TM_MLP
    const = lambda shape: pl.BlockSpec(shape, lambda i: (0,) * len(shape),
                                       pipeline_mode=pl.Buffered(1))
    return pl.pallas_call(
        _mlp_kernel,
        grid=(n // tm,),
        in_specs=[
            pl.BlockSpec((tm, D_MODEL), lambda i: (i, 0)),
            pl.BlockSpec((tm, POOL_WIDTH), lambda i: (i, 0)),
            pl.BlockSpec((N_PAIRS, tm // MAX_DIL, MAX_DIL * LANES), lambda i: (0, i, 0)),
            const((D_MODEL, D_MODEL)),
            const((1, D_MODEL)),
            const((D_MODEL, D_FF)),
            const((D_FF, D_MODEL)),
        ],
        out_specs=pl.BlockSpec((tm, D_MODEL), lambda i: (i, 0)),
        out_shape=jax.ShapeDtypeStruct((n, D_MODEL), F32),
        scratch_shapes=[pltpu.VMEM((N_PAIRS, tm, LANES), F32)],
        compiler_params=pltpu.CompilerParams(dimension_semantics=("arbitrary",),
                                             vmem_limit_bytes=VMEM_LIMIT),
        name="mlp",
    )(x2, ypool, yattn, w_out, mlp_g, w_up, w_down)


def kernel(x, mix_norm_g, w_in, pool_w, pool_scale, q_norm_g, k_norm_g, rel_bias,
           w_out, mlp_norm_g, w_up, w_down):
    batch, seq, d = x.shape
    assert d == D_MODEL and seq % SUPER == 0 and seq % TM_PROJ == 0
    x2 = x.reshape(batch * seq, d)
    tile_heads = lambda g: jnp.tile(g.astype(F32), N_HEADS).reshape(1, ATTN_WIDTH)

    ypool, q, k, v = _proj_call(
        x2, mix_norm_g.reshape(1, d), w_in.astype(BF16),
        tile_heads(q_norm_g) * (HEAD_DIM ** -0.5), tile_heads(k_norm_g), pool_w.astype(BF16),
        pool_scale.reshape(1, POOL_WIDTH), seq)
    bias = _bias_call(rel_bias)
    yattn = _attn_call(q, k, v, bias, batch, seq)
    y = _mlp_call(x2, ypool, yattn, w_out.astype(BF16), mlp_norm_g.reshape(1, d),
                  w_up.astype(BF16), w_down.astype(BF16))
    return y.reshape(batch, seq, d)
```

```python
import functools
import math

import numpy as np
import jax
import jax.numpy as jnp
from jax import lax
from jax.experimental import pallas as pl
from jax.experimental.pallas import tpu as pltpu

D_MODEL = 1024
POOL_WIDTH = 512
POOL_WINDOWS = (2, 4, 8, 16)
POOL_GROUP_DIM = 128
ATTN_WIDTH = 512
HEAD_DIM = 64
N_HEADS = 8
DILATIONS = (1, 4, 16)
WINDOW_STEPS = 128
N_BUCKETS = 32
MAX_DISTANCE = 2048
D_FF = 4096
NORM_EPS = 1e-6
NEG_INF = -1e30

LANES = 128
N_PAIRS = ATTN_WIDTH // LANES
MAX_DIL = 16
SUPER = WINDOW_STEPS * MAX_DIL
HALO = 32
TM_PROJ = 1024
RELAY_PITCH = TM_PROJ // MAX_DIL + 8
TM_MLP = 1024
FF_CHUNK = 1024
VMEM_LIMIT = 56 * 1024 * 1024

F32 = jnp.float32
BF16 = jnp.bfloat16


def _rms(x, g):
    return x * lax.rsqrt(jnp.mean(x * x, axis=-1, keepdims=True) + NORM_EPS) * g


def _proj_kernel(x_ref, g_ref, w32_ref, qg_ref, kg_ref, pw_ref, ps_ref, wo32_ref, wu32_ref, wd32_ref,
                 ypool_ref, q_ref, k_ref, v_ref, wo_ref, wu_ref, wd_ref, w_ref, ubuf, sbuf, relay,
                 *, tiles_per_seq):
    i = pl.program_id(0)
    tm = x_ref.shape[0]

    @pl.when(i == 0)
    def _():
        w_ref[...] = w32_ref[...].astype(BF16)

    wo_ref[...] = wo32_ref[...].astype(BF16)
    wu_ref[...] = wu32_ref[...].astype(BF16)
    wd_ref[...] = wd32_ref[...].astype(BF16)

    seq_tile = i % tiles_per_seq

    @pl.when(seq_tile == 0)
    def _():
        ubuf[0:HALO, :] = jnp.zeros((HALO, POOL_WIDTH), F32)

    @pl.when(seq_tile != 0)
    def _():
        ubuf[0:HALO, :] = ubuf[tm:tm + HALO, :]

    a = _rms(x_ref[...], g_ref[...]).astype(BF16)

    def head_norm(t, gain):
        sq = t * t
        head0 = lax.broadcasted_iota(jnp.int32, (tm, LANES), 1) < HEAD_DIM
        zero = jnp.zeros((tm, LANES), F32)
        ssq = []
        for c in range(0, ATTN_WIDTH, LANES):
            s = sq[:, c:c + LANES]
            s0 = jnp.sum(jnp.where(head0, s, zero), axis=-1, keepdims=True)
            s1 = jnp.sum(jnp.where(head0, zero, s), axis=-1, keepdims=True)
            ssq.append(jnp.where(head0, s0, s1))
        ms = jnp.concatenate(ssq, axis=-1) * (1.0 / HEAD_DIM)
        return t * lax.rsqrt(ms + NORM_EPS) * gain

    def put_pairs(ref, which, t):
        rows = tm // MAX_DIL
        for hp in range(N_PAIRS):
            for g in range(tm // 8):
                j, r0 = g // 2, 8 * (g % 2)
                relay[which, hp, pl.ds(r0 * RELAY_PITCH + j, 8, stride=RELAY_PITCH), :] = (
                    t[8 * g:8 * g + 8, hp * LANES:(hp + 1) * LANES])
        for hp in range(N_PAIRS):
            for r in range(MAX_DIL):
                ref[hp, :, r * LANES:(r + 1) * LANES] = (
                    relay[which, hp, r * RELAY_PITCH:r * RELAY_PITCH + rows, :].astype(BF16))

    u = jnp.dot(a, w_ref[:, :POOL_WIDTH], preferred_element_type=F32)
    ubuf[HALO:HALO + tm, :] = u
    pos = seq_tile * tm + lax.broadcasted_iota(jnp.int32, (tm, 1), 0)
    pooled = []
    end = HALO + tm
    for g, w in enumerate(POOL_WINDOWS):
        cols = slice(g * POOL_GROUP_DIM, (g + 1) * POOL_GROUP_DIM)
        levels = int(math.log2(w))
        src, span = ubuf, 1
        for lvl in range(levels):
            lo = HALO if lvl == levels - 1 else HALO - 8 * (levels - 1 - lvl)
            summed = src[lo:end, cols] + src[lo - span:end - span, cols]
            span *= 2
            if lvl == levels - 1:
                wsum = summed
            else:
                sbuf[lvl, lo:end, cols] = summed
                src = sbuf.at[lvl]
        inv_count = 1.0 / jnp.minimum(pos + 1, w).astype(F32)
        pooled.append((wsum * inv_count - ubuf[HALO:end, cols]).astype(BF16))

    q = jnp.dot(a, w_ref[:, POOL_WIDTH:POOL_WIDTH + ATTN_WIDTH], preferred_element_type=F32)
    put_pairs(q_ref, 0, head_norm(q, qg_ref[...]))
    k = jnp.dot(a, w_ref[:, POOL_WIDTH + ATTN_WIDTH:POOL_WIDTH + 2 * ATTN_WIDTH],
                preferred_element_type=F32)
    put_pairs(k_ref, 1, head_norm(k, kg_ref[...]))
    v = jnp.dot(a, w_ref[:, POOL_WIDTH + 2 * ATTN_WIDTH:], preferred_element_type=F32)
    put_pairs(v_ref, 2, v)

    for g in range(len(POOL_WINDOWS)):
        cols = slice(g * POOL_GROUP_DIM, (g + 1) * POOL_GROUP_DIM)
        mixed = jnp.dot(pooled[g], pw_ref[g], preferred_element_type=F32)
        ypool_ref[:, cols] = (mixed * ps_ref[:, cols]).astype(BF16)


def _proj_call(x2, mix_g, w_in, qg, kg, pool_w, pool_scale, later_weights, seq):
    n = x2.shape[0]
    tm = TM_PROJ
    steps = n // tm
    const = lambda shape, **kw: pl.BlockSpec(shape, lambda i: (0,) * len(shape), **kw)
    row_slice = lambda wgt: pl.BlockSpec((wgt.shape[0] // steps, wgt.shape[1]), lambda i: (i, 0))
    pair_spec = pl.BlockSpec((N_PAIRS, tm // MAX_DIL, MAX_DIL * LANES), lambda i: (0, i, 0))
    pair_shape = jax.ShapeDtypeStruct((N_PAIRS, n // MAX_DIL, MAX_DIL * LANES), BF16)
    return pl.pallas_call(
        functools.partial(_proj_kernel, tiles_per_seq=seq // tm),
        grid=(n // tm,),
        in_specs=[
            pl.BlockSpec((tm, D_MODEL), lambda i: (i, 0)),
            const((1, D_MODEL)),
            const((D_MODEL, POOL_WIDTH + 3 * ATTN_WIDTH), pipeline_mode=pl.Buffered(1)),
            const((1, ATTN_WIDTH)),
            const((1, ATTN_WIDTH)),
            const((len(POOL_WINDOWS), POOL_GROUP_DIM, POOL_GROUP_DIM)),
            const((1, POOL_WIDTH)),
        ] + [row_slice(wgt) for wgt in later_weights],
        out_specs=[pl.BlockSpec((tm, POOL_WIDTH), lambda i: (i, 0)), pair_spec, pair_spec, pair_spec]
        + [row_slice(wgt) for wgt in later_weights],
        out_shape=[jax.ShapeDtypeStruct((n, POOL_WIDTH), BF16), pair_shape, pair_shape, pair_shape]
        + [jax.ShapeDtypeStruct(wgt.shape, BF16) for wgt in later_weights],
        scratch_shapes=[pltpu.VMEM((D_MODEL, POOL_WIDTH + 3 * ATTN_WIDTH), BF16),
                        pltpu.VMEM((HALO + tm, POOL_WIDTH), F32),
                        pltpu.VMEM((3, HALO + tm, POOL_WIDTH), F32),
                        pltpu.VMEM((3, N_PAIRS, MAX_DIL * RELAY_PITCH, LANES), F32)],
        compiler_params=pltpu.CompilerParams(dimension_semantics=("arbitrary",),
                                             vmem_limit_bytes=VMEM_LIMIT),
        name="proj",
    )(x2, mix_g, w_in, qg, kg, pool_w, pool_scale, *later_weights)


def _block_orders():
    out = []
    for dil in DILATIONS:
        ns = MAX_DIL // dil
        c = WINDOW_STEPS // ns
        slab = np.arange(ns)[:, None]
        q_step = (np.arange(c)[None, :] * ns + slab + WINDOW_STEPS).reshape(-1)
        k_step = np.concatenate([q_step - WINDOW_STEPS, q_step])
        out.append((dil, ns, c, q_step, k_step))
    return out


def _bucket_tables():
    max_exact = N_BUCKETS // 2
    buckets, prev_cols = [], []
    for dil, ns, c, q_step, k_step in _block_orders():
        dist = q_step[:, None] - k_step[None, :]
        ok = (dist >= 0) & (dist <= WINDOW_STEPS)
        tok = np.clip(dist, 0, WINDOW_STEPS) * dil
        d_f = np.maximum(tok, 1).astype(np.float32)
        large = max_exact + (np.log(d_f / np.float32(max_exact)) / np.float32(math.log(MAX_DISTANCE / max_exact))
                             * np.float32(N_BUCKETS - max_exact)).astype(np.int32)
        large = np.minimum(large, N_BUCKETS - 1)
        bucket = np.where(tok < max_exact, tok, large)
        buckets.append(np.where(ok, bucket, -1).astype(np.int32))
        prev_cols.append(np.broadcast_to((k_step < WINDOW_STEPS)[None, :], dist.shape).astype(np.int32))
    return np.stack(buckets), np.stack(prev_cols)


def _bias_kernel(rel_ref, bucket_ref, prev_ref, out_ref):
    bucket = bucket_ref[0]
    is_prev = prev_ref[0] != 0
    for h in range(N_HEADS):
        tab = jnp.full(bucket.shape, NEG_INF, F32)
        for b in range(N_BUCKETS):
            tab = jnp.where(bucket == b, rel_ref[b, h], tab)
        rows = slice((h % 2) * WINDOW_STEPS, (h % 2 + 1) * WINDOW_STEPS)
        out_ref[0, 0, h // 2, rows, :] = tab
        out_ref[0, 1, h // 2, rows, :] = jnp.where(is_prev, NEG_INF, tab)


def _bias_call(rel_bias):
    bucket, prev_cols = _bucket_tables()
    npat = len(DILATIONS)
    tab_spec = pl.BlockSpec((1, WINDOW_STEPS, 2 * WINDOW_STEPS), lambda p: (p, 0, 0))
    return pl.pallas_call(
        _bias_kernel,
        grid=(npat,),
        in_specs=[pl.BlockSpec(memory_space=pltpu.SMEM), tab_spec, tab_spec],
        out_specs=pl.BlockSpec((1, 2, N_PAIRS, 2 * WINDOW_STEPS, 2 * WINDOW_STEPS),
                               lambda p: (p, 0, 0, 0, 0)),
        out_shape=jax.ShapeDtypeStruct((npat, 2, N_PAIRS, 2 * WINDOW_STEPS, 2 * WINDOW_STEPS), F32),
        name="bias",
    )(rel_bias, jnp.asarray(bucket), jnp.asarray(prev_cols))


TILES = 2


def _pattern_blocks(p):
    dil = DILATIONS[p]
    ns = MAX_DIL // dil
    c = WINDOW_STEPS // ns
    return [(a * ns + n, [a + dil * j for j in range(ns)], c * n, c, a, n)
            for a in range(dil) for n in range(ns)]


def _attn_kernel(q_ref, k_ref, v_ref, bias_ref, o_ref,
                 qa32, qb32, k32, v32, kc_s, vc_s, kcarry, vcarry, acc_s, m_s, l_s, *, supers_per_seq):
    step = pl.program_id(1)
    is_first = (step * TILES) % supers_per_seq == 0
    first = jnp.where(is_first, 1, 0)
    w = WINDOW_STEPS
    npat = len(DILATIONS)

    @pl.when(is_first)
    def _():
        for p in range(npat):
            for a in range(DILATIONS[p]):
                kcarry[p, a] = jnp.zeros((w, LANES), BF16)
                vcarry[p, a] = jnp.zeros((w, LANES), BF16)

    @pl.when(jnp.logical_not(is_first))
    def _():
        for p in range(npat):
            ns = MAX_DIL // DILATIONS[p]
            for a in range(DILATIONS[p]):
                kcarry[p, a] = kc_s[TILES - 1, p, a * ns + ns - 1]
                vcarry[p, a] = vc_s[TILES - 1, p, a * ns + ns - 1]

    head0 = lax.broadcasted_iota(jnp.int32, (w, LANES), 1) < HEAD_DIM

    def load_slabs(t):
        rows = slice(t * w, (t + 1) * w)
        for r in range(MAX_DIL):
            cols = slice(r * LANES, (r + 1) * LANES)
            q = q_ref[rows, cols].astype(F32)
            zero = jnp.zeros_like(q)
            qa32[t, r] = jnp.where(head0, q, zero)
            qb32[t, r] = jnp.where(head0, zero, q)
            k32[t, r] = k_ref[rows, cols].astype(F32)
            v32[t, r] = v_ref[rows, cols].astype(F32)

    def gather(ref, t, slabs, row0, c):
        return jnp.concatenate([ref[t, s, row0:row0 + c, :] for s in slabs], axis=0)

    def attend(t, p, blk):
        b, slabs, row0, c, stream, n = blk
        ns = MAX_DIL // DILATIONS[p]
        k_cur = gather(k32, t, slabs, row0, c).astype(BF16)
        v_cur = gather(v32, t, slabs, row0, c).astype(BF16)
        qm = jnp.concatenate([gather(qa32, t, slabs, row0, c), gather(qb32, t, slabs, row0, c)],
                             axis=0).astype(BF16)
        if n:
            k_prev, v_prev = kc_s[t, p, b - 1], vc_s[t, p, b - 1]
        elif t:
            k_prev, v_prev = kc_s[t - 1, p, b + ns - 1], vc_s[t - 1, p, b + ns - 1]
        else:
            k_prev, v_prev = kcarry[p, stream], vcarry[p, stream]
        kk = jnp.concatenate([k_prev, k_cur], axis=0)
        vv = jnp.concatenate([v_prev, v_cur], axis=0)
        kc_s[t, p, b] = k_cur
        vc_s[t, p, b] = v_cur
        bias = bias_ref[p, first] if (n == 0 and t == 0) else bias_ref[p, 0]
        s = lax.dot_general(qm, kk, (((1,), (1,)), ((), ())), preferred_element_type=F32) + bias
        m = jnp.max(s, axis=-1, keepdims=True)
        e = jnp.exp(s - m)
        l = jnp.sum(e, axis=-1, keepdims=True)
        pv = jnp.dot(e.astype(BF16), vv, preferred_element_type=F32)
        acc = jnp.where(head0, pv[:w], pv[w:])
        m_b = jnp.where(head0, m[:w], m[w:])
        l_b = jnp.where(head0, l[:w], l[w:])
        for j, sl in enumerate(slabs):
            acc_s[t, p, sl, row0:row0 + c, :] = acc[j * c:(j + 1) * c, :]
            m_s[t, p, sl, row0:row0 + c, :] = m_b[j * c:(j + 1) * c, :]
            l_s[t, p, sl, row0:row0 + c, :] = l_b[j * c:(j + 1) * c, :]

    def combine(t, r):
        m0, m1, m2 = m_s[t, 0, r], m_s[t, 1, r], m_s[t, 2, r]
        m_all = jnp.maximum(jnp.maximum(m0, m1), m2)
        w0, w1, w2 = jnp.exp(m0 - m_all), jnp.exp(m1 - m_all), jnp.exp(m2 - m_all)
        num = w0 * acc_s[t, 0, r] + w1 * acc_s[t, 1, r] + w2 * acc_s[t, 2, r]
        den = w0 * l_s[t, 0, r] + w1 * l_s[t, 1, r] + w2 * l_s[t, 2, r]
        o_ref[t * w:(t + 1) * w, r * LANES:(r + 1) * LANES] = (num / den).astype(BF16)

    for t in range(TILES):
        load_slabs(t)
    for t in range(TILES):
        for p in reversed(range(npat)):
            for blk in _pattern_blocks(p):
                attend(t, p, blk)
        for r in range(MAX_DIL):
            combine(t, r)


def _attn_call(q, k, v, bias, batch, seq):
    rows, width = q.shape[1:]
    supers_per_seq = seq // SUPER
    assert supers_per_seq % TILES == 0
    w = WINDOW_STEPS
    tok = pl.BlockSpec((None, TILES * w, width), lambda hp, st: (hp, st, 0))
    npat = len(DILATIONS)
    slab = pltpu.VMEM((TILES, MAX_DIL, w, LANES), F32)
    part = pltpu.VMEM((TILES, npat, MAX_DIL, w, LANES), F32)
    chunks = pltpu.VMEM((TILES, npat, MAX_DIL, w, LANES), BF16)
    carry = pltpu.VMEM((npat, MAX_DIL, w, LANES), BF16)
    return pl.pallas_call(
        functools.partial(_attn_kernel, supers_per_seq=supers_per_seq),
        grid=(N_PAIRS, rows // (TILES * w)),
        in_specs=[tok, tok, tok,
                  pl.BlockSpec((npat, 2, None, 2 * w, 2 * w), lambda hp, st: (0, 0, hp, 0, 0))],
        out_specs=tok,
        out_shape=jax.ShapeDtypeStruct((N_PAIRS, rows, width), BF16),
        scratch_shapes=[slab, slab, slab, slab, chunks, chunks, carry, carry, part, part, part],
        compiler_params=pltpu.CompilerParams(dimension_semantics=("arbitrary", "arbitrary"),
                                             vmem_limit_bytes=VMEM_LIMIT),
        name="attn",
    )(q, k, v, bias)


def _mlp_kernel(x_ref, ypool_ref, yattn_ref, wo_ref, g_ref, wu_ref, wd_ref, o_ref, relay):
    tm = x_ref.shape[0]
    for hp in range(N_PAIRS):
        for r in range(MAX_DIL):
            relay[hp, pl.ds(r, tm // MAX_DIL, stride=MAX_DIL), :] = (
                yattn_ref[hp, :, r * LANES:(r + 1) * LANES].astype(F32))
    mixed = jnp.concatenate([ypool_ref[...]] + [relay[hp].astype(BF16) for hp in range(N_PAIRS)], axis=-1)
    h = x_ref[...] + jnp.dot(mixed, wo_ref[...], preferred_element_type=F32)
    c = _rms(h, g_ref[...]).astype(BF16)
    acc = h
    for f in range(0, D_FF, FF_CHUNK):
        up = jnp.dot(c, wu_ref[:, f:f + FF_CHUNK], preferred_element_type=F32)
        ff = jnp.square(jnp.maximum(up, 0.0)).astype(BF16)
        acc = acc + jnp.dot(ff, wd_ref[f:f + FF_CHUNK, :], preferred_element_type=F32)
    o_ref[...] = acc


def _mlp_call(x2, ypool, yattn, w_out, mlp_g, w_up, w_down):
    n = x2.shape[0]
    tm = TM_MLP
    const = lambda shape: pl.BlockSpec(shape, lambda i: (0,) * len(shape),
                                       pipeline_mode=pl.Buffered(1))
    return pl.pallas_call(
        _mlp_kernel,
        grid=(n // tm,),
        in_specs=[
            pl.BlockSpec((tm, D_MODEL), lambda i: (i, 0)),
            pl.BlockSpec((tm, POOL_WIDTH), lambda i: (i, 0)),
            pl.BlockSpec((N_PAIRS, tm // MAX_DIL, MAX_DIL * LANES), lambda i: (0, i, 0)),
            const((D_MODEL, D_MODEL)),
            const((1, D_MODEL)),
            const((D_MODEL, D_FF)),
            const((D_FF, D_MODEL)),
        ],
        out_specs=pl.BlockSpec((tm, D_MODEL), lambda i: (i, 0)),
        out_shape=jax.ShapeDtypeStruct((n, D_MODEL), F32),
        scratch_shapes=[pltpu.VMEM((N_PAIRS, tm, LANES), F32)],
        compiler_params=pltpu.CompilerParams(dimension_semantics=("arbitrary",),
                                             vmem_limit_bytes=VMEM_LIMIT),
        name="mlp",
    )(x2, ypool, yattn, w_out, mlp_g, w_up, w_down)


def kernel(x, mix_norm_g, w_in, pool_w, pool_scale, q_norm_g, k_norm_g, rel_bias,
           w_out, mlp_norm_g, w_up, w_down):
    batch, seq, d = x.shape
    assert d == D_MODEL and seq % SUPER == 0 and seq % TM_PROJ == 0
    x2 = x.reshape(batch * seq, d)
    tile_heads = lambda g: jnp.tile(g.astype(F32), N_HEADS).reshape(1, ATTN_WIDTH)

    ypool, q, k, v, w_out16, w_up16, w_down16 = _proj_call(
        x2, mix_norm_g.reshape(1, d), w_in,
        tile_heads(q_norm_g) * (HEAD_DIM ** -0.5), tile_heads(k_norm_g), pool_w.astype(BF16),
        pool_scale.reshape(1, POOL_WIDTH), (w_out, w_up, w_down), seq)
    bias = _bias_call(rel_bias)
    yattn = _attn_call(q, k, v, bias, batch, seq)
    y = _mlp_call(x2, ypool, yattn, w_out16, mlp_norm_g.reshape(1, d), w_up16, w_down16)
    return y.reshape(batch, seq, d)
```

```python
import functools
import math

import numpy as np
import jax
import jax.numpy as jnp
from jax import lax
from jax.experimental import pallas as pl
from jax.experimental.pallas import tpu as pltpu

D_MODEL = 1024
POOL_WIDTH = 512
POOL_WINDOWS = (2, 4, 8, 16)
POOL_GROUP_DIM = 128
ATTN_WIDTH = 512
HEAD_DIM = 64
N_HEADS = 8
DILATIONS = (1, 4, 16)
WINDOW_STEPS = 128
N_BUCKETS = 32
MAX_DISTANCE = 2048
D_FF = 4096
NORM_EPS = 1e-6
NEG_INF = -1e30

LANES = 128
N_PAIRS = ATTN_WIDTH // LANES
MAX_DIL = 16
SUPER = WINDOW_STEPS * MAX_DIL
HALO = 32
TM_PROJ = 1024
RELAY_PITCH = TM_PROJ // MAX_DIL + 8
TM_MLP = 1024
FF_CHUNK = 1024
VMEM_LIMIT = 56 * 1024 * 1024

F32 = jnp.float32
BF16 = jnp.bfloat16


def _rms(x, g):
    return x * lax.rsqrt(jnp.mean(x * x, axis=-1, keepdims=True) + NORM_EPS) * g


def _proj_kernel(x_ref, g_ref, w32_ref, qg_ref, kg_ref, pw_ref, ps_ref, wo32_ref, wu32_ref, wd32_ref,
                 ypool_ref, q_ref, k_ref, v_ref, wo_ref, wu_ref, wd_ref, w_ref, ubuf, sbuf, relay,
                 *, tiles_per_seq):
    i = pl.program_id(0)
    tm = x_ref.shape[0]

    @pl.when(i == 0)
    def _():
        w_ref[...] = w32_ref[...].astype(BF16)

    wo_ref[...] = wo32_ref[...].astype(BF16)
    wu_ref[...] = wu32_ref[...].astype(BF16)
    wd_ref[...] = wd32_ref[...].astype(BF16)

    seq_tile = i % tiles_per_seq

    @pl.when(seq_tile == 0)
    def _():
        ubuf[0:HALO, :] = jnp.zeros((HALO, POOL_WIDTH), F32)

    @pl.when(seq_tile != 0)
    def _():
        ubuf[0:HALO, :] = ubuf[tm:tm + HALO, :]

    a = _rms(x_ref[...], g_ref[...]).astype(BF16)

    def head_norm(t, gain):
        sq = t * t
        head0 = lax.broadcasted_iota(jnp.int32, (tm, LANES), 1) < HEAD_DIM
        zero = jnp.zeros((tm, LANES), F32)
        ssq = []
        for c in range(0, ATTN_WIDTH, LANES):
            s = sq[:, c:c + LANES]
            s0 = jnp.sum(jnp.where(head0, s, zero), axis=-1, keepdims=True)
            s1 = jnp.sum(jnp.where(head0, zero, s), axis=-1, keepdims=True)
            ssq.append(jnp.where(head0, s0, s1))
        ms = jnp.concatenate(ssq, axis=-1) * (1.0 / HEAD_DIM)
        return t * lax.rsqrt(ms + NORM_EPS) * gain

    def put_pairs(ref, which, t):
        rows = tm // MAX_DIL
        for hp in range(N_PAIRS):
            for g in range(tm // 8):
                j, r0 = g // 2, 8 * (g % 2)
                relay[which, hp, pl.ds(r0 * RELAY_PITCH + j, 8, stride=RELAY_PITCH), :] = (
                    t[8 * g:8 * g + 8, hp * LANES:(hp + 1) * LANES])
        for hp in range(N_PAIRS):
            for r in range(MAX_DIL):
                ref[hp, :, r * LANES:(r + 1) * LANES] = (
                    relay[which, hp, r * RELAY_PITCH:r * RELAY_PITCH + rows, :].astype(BF16))

    u = jnp.dot(a, w_ref[:, :POOL_WIDTH], preferred_element_type=F32)
    ubuf[HALO:HALO + tm, :] = u
    pos = seq_tile * tm + lax.broadcasted_iota(jnp.int32, (tm, 1), 0)
    pooled = []
    end = HALO + tm
    for g, w in enumerate(POOL_WINDOWS):
        cols = slice(g * POOL_GROUP_DIM, (g + 1) * POOL_GROUP_DIM)
        levels = int(math.log2(w))
        src, span = ubuf, 1
        for lvl in range(levels):
            lo = HALO if lvl == levels - 1 else HALO - 8 * (levels - 1 - lvl)
            summed = src[lo:end, cols] + src[lo - span:end - span, cols]
            span *= 2
            if lvl == levels - 1:
                wsum = summed
            else:
                sbuf[lvl, lo:end, cols] = summed
                src = sbuf.at[lvl]
        inv_count = 1.0 / jnp.minimum(pos + 1, w).astype(F32)
        pooled.append((wsum * inv_count - ubuf[HALO:end, cols]).astype(BF16))

    q = jnp.dot(a, w_ref[:, POOL_WIDTH:POOL_WIDTH + ATTN_WIDTH], preferred_element_type=F32)
    put_pairs(q_ref, 0, head_norm(q, qg_ref[...]))
    k = jnp.dot(a, w_ref[:, POOL_WIDTH + ATTN_WIDTH:POOL_WIDTH + 2 * ATTN_WIDTH],
                preferred_element_type=F32)
    put_pairs(k_ref, 1, head_norm(k, kg_ref[...]))
    v = jnp.dot(a, w_ref[:, POOL_WIDTH + 2 * ATTN_WIDTH:], preferred_element_type=F32)
    put_pairs(v_ref, 2, v)

    for g in range(len(POOL_WINDOWS)):
        cols = slice(g * POOL_GROUP_DIM, (g + 1) * POOL_GROUP_DIM)
        mixed = jnp.dot(pooled[g], pw_ref[g], preferred_element_type=F32)
        ypool_ref[:, cols] = (mixed * ps_ref[:, cols]).astype(BF16)


def _proj_call(x2, mix_g, w_in, qg, kg, pool_w, pool_scale, later_weights, seq):
    n = x2.shape[0]
    tm = TM_PROJ
    steps = n // tm
    const = lambda shape, **kw: pl.BlockSpec(shape, lambda i: (0,) * len(shape), **kw)
    row_slice = lambda wgt: pl.BlockSpec((wgt.shape[0] // steps, wgt.shape[1]), lambda i: (i, 0))
    pair_spec = pl.BlockSpec((N_PAIRS, tm // MAX_DIL, MAX_DIL * LANES), lambda i: (0, i, 0))
    pair_shape = jax.ShapeDtypeStruct((N_PAIRS, n // MAX_DIL, MAX_DIL * LANES), BF16)
    return pl.pallas_call(
        functools.partial(_proj_kernel, tiles_per_seq=seq // tm),
        grid=(n // tm,),
        in_specs=[
            pl.BlockSpec((tm, D_MODEL), lambda i: (i, 0)),
            const((1, D_MODEL)),
            const((D_MODEL, POOL_WIDTH + 3 * ATTN_WIDTH), pipeline_mode=pl.Buffered(1)),
            const((1, ATTN_WIDTH)),
            const((1, ATTN_WIDTH)),
            const((len(POOL_WINDOWS), POOL_GROUP_DIM, POOL_GROUP_DIM)),
            const((1, POOL_WIDTH)),
        ] + [row_slice(wgt) for wgt in later_weights],
        out_specs=[pl.BlockSpec((tm, POOL_WIDTH), lambda i: (i, 0)), pair_spec, pair_spec, pair_spec]
        + [row_slice(wgt) for wgt in later_weights],
        out_shape=[jax.ShapeDtypeStruct((n, POOL_WIDTH), BF16), pair_shape, pair_shape, pair_shape]
        + [jax.ShapeDtypeStruct(wgt.shape, BF16) for wgt in later_weights],
        scratch_shapes=[pltpu.VMEM((D_MODEL, POOL_WIDTH + 3 * ATTN_WIDTH), BF16),
                        pltpu.VMEM((HALO + tm, POOL_WIDTH), F32),
                        pltpu.VMEM((3, HALO + tm, POOL_WIDTH), F32),
                        pltpu.VMEM((3, N_PAIRS, MAX_DIL * RELAY_PITCH, LANES), F32)],
        compiler_params=pltpu.CompilerParams(dimension_semantics=("arbitrary",),
                                             vmem_limit_bytes=VMEM_LIMIT),
        name="proj",
    )(x2, mix_g, w_in, qg, kg, pool_w, pool_scale, *later_weights)


def _block_orders():
    out = []
    for dil in DILATIONS:
        ns = MAX_DIL // dil
        c = WINDOW_STEPS // ns
        slab = np.arange(ns)[:, None]
        q_step = (np.arange(c)[None, :] * ns + slab + WINDOW_STEPS).reshape(-1)
        k_step = np.concatenate([q_step - WINDOW_STEPS, q_step])
        out.append((dil, ns, c, q_step, k_step))
    return out


def _bucket_tables():
    max_exact = N_BUCKETS // 2
    buckets, prev_cols = [], []
    for dil, ns, c, q_step, k_step in _block_orders():
        dist = q_step[:, None] - k_step[None, :]
        ok = (dist >= 0) & (dist <= WINDOW_STEPS)
        tok = np.clip(dist, 0, WINDOW_STEPS) * dil
        d_f = np.maximum(tok, 1).astype(np.float32)
        large = max_exact + (np.log(d_f / np.float32(max_exact)) / np.float32(math.log(MAX_DISTANCE / max_exact))
                             * np.float32(N_BUCKETS - max_exact)).astype(np.int32)
        large = np.minimum(large, N_BUCKETS - 1)
        bucket = np.where(tok < max_exact, tok, large)
        buckets.append(np.where(ok, bucket, -1).astype(np.int32))
        prev_cols.append(np.broadcast_to((k_step < WINDOW_STEPS)[None, :], dist.shape).astype(np.int32))
    return np.stack(buckets), np.stack(prev_cols)


def _bias_kernel(rel_ref, bucket_ref, prev_ref, out_ref):
    bucket = bucket_ref[0]
    is_prev = prev_ref[0] != 0
    for h in range(N_HEADS):
        tab = jnp.full(bucket.shape, NEG_INF, F32)
        for b in range(N_BUCKETS):
            tab = jnp.where(bucket == b, rel_ref[b, h], tab)
        rows = slice((h % 2) * WINDOW_STEPS, (h % 2 + 1) * WINDOW_STEPS)
        out_ref[0, 0, h // 2, rows, :] = tab
        out_ref[0, 1, h // 2, rows, :] = jnp.where(is_prev, NEG_INF, tab)


def _bias_call(rel_bias):
    bucket, prev_cols = _bucket_tables()
    npat = len(DILATIONS)
    tab_spec = pl.BlockSpec((1, WINDOW_STEPS, 2 * WINDOW_STEPS), lambda p: (p, 0, 0))
    return pl.pallas_call(
        _bias_kernel,
        grid=(npat,),
        in_specs=[pl.BlockSpec(memory_space=pltpu.SMEM), tab_spec, tab_spec],
        out_specs=pl.BlockSpec((1, 2, N_PAIRS, 2 * WINDOW_STEPS, 2 * WINDOW_STEPS),
                               lambda p: (p, 0, 0, 0, 0)),
        out_shape=jax.ShapeDtypeStruct((npat, 2, N_PAIRS, 2 * WINDOW_STEPS, 2 * WINDOW_STEPS), F32),
        name="bias",
    )(rel_bias, jnp.asarray(bucket), jnp.asarray(prev_cols))


TILES = 2


def _pattern_blocks(p):
    dil = DILATIONS[p]
    ns = MAX_DIL // dil
    c = WINDOW_STEPS // ns
    return [(a * ns + n, [a + dil * j for j in range(ns)], c * n, c, a, n)
            for a in range(dil) for n in range(ns)]


def _attn_kernel(q_ref, k_ref, v_ref, bias_ref, o_ref,
                 qa32, qb32, k32, v32, kc_s, vc_s, kcarry, vcarry, acc_s, m_s, l_s, *, supers_per_seq):
    step = pl.program_id(1)
    is_first = (step * TILES) % supers_per_seq == 0
    first = jnp.where(is_first, 1, 0)
    w = WINDOW_STEPS
    npat = len(DILATIONS)

    @pl.when(is_first)
    def _():
        for p in range(npat):
            for a in range(DILATIONS[p]):
                kcarry[p, a] = jnp.zeros((w, LANES), BF16)
                vcarry[p, a] = jnp.zeros((w, LANES), BF16)

    @pl.when(jnp.logical_not(is_first))
    def _():
        for p in range(npat):
            ns = MAX_DIL // DILATIONS[p]
            for a in range(DILATIONS[p]):
                kcarry[p, a] = kc_s[TILES - 1, p, a * ns + ns - 1]
                vcarry[p, a] = vc_s[TILES - 1, p, a * ns + ns - 1]

    head0 = lax.broadcasted_iota(jnp.int32, (w, LANES), 1) < HEAD_DIM

    def load_slabs(t):
        rows = slice(t * w, (t + 1) * w)
        for r in range(MAX_DIL):
            cols = slice(r * LANES, (r + 1) * LANES)
            q = q_ref[rows, cols].astype(F32)
            zero = jnp.zeros_like(q)
            qa32[t, r] = jnp.where(head0, q, zero)
            qb32[t, r] = jnp.where(head0, zero, q)
            k32[t, r] = k_ref[rows, cols].astype(F32)
            v32[t, r] = v_ref[rows, cols].astype(F32)

    def gather(ref, t, slabs, row0, c):
        return jnp.concatenate([ref[t, s, row0:row0 + c, :] for s in slabs], axis=0)

    def attend(t, p, blk):
        b, slabs, row0, c, stream, n = blk
        ns = MAX_DIL // DILATIONS[p]
        k_cur = gather(k32, t, slabs, row0, c).astype(BF16)
        v_cur = gather(v32, t, slabs, row0, c).astype(BF16)
        qm = jnp.concatenate([gather(qa32, t, slabs, row0, c), gather(qb32, t, slabs, row0, c)],
                             axis=0).astype(BF16)
        if n:
            k_prev, v_prev = kc_s[t, p, b - 1], vc_s[t, p, b - 1]
        elif t:
            k_prev, v_prev = kc_s[t - 1, p, b + ns - 1], vc_s[t - 1, p, b + ns - 1]
        else:
            k_prev, v_prev = kcarry[p, stream], vcarry[p, stream]
        kk = jnp.concatenate([k_prev, k_cur], axis=0)
        vv = jnp.concatenate([v_prev, v_cur], axis=0)
        kc_s[t, p, b] = k_cur
        vc_s[t, p, b] = v_cur
        bias = bias_ref[p, first] if (n == 0 and t == 0) else bias_ref[p, 0]
        s = lax.dot_general(qm, kk, (((1,), (1,)), ((), ())), preferred_element_type=F32) + bias
        m = jnp.max(s, axis=-1, keepdims=True)
        e = jnp.exp(s - m).astype(BF16)
        pv = jnp.dot(e, jnp.concatenate([vv, jnp.ones_like(vv)], axis=1), preferred_element_type=F32)
        for hh in range(2):
            lanes = slice(hh * HEAD_DIM, (hh + 1) * HEAD_DIM)
            m_h = jnp.broadcast_to(m[hh * w:(hh + 1) * w], (w, HEAD_DIM))
            for j, sl in enumerate(slabs):
                src = slice(hh * w + j * c, hh * w + (j + 1) * c)
                acc_s[t, p, sl, row0:row0 + c, lanes] = pv[src, lanes]
                m_s[t, p, sl, row0:row0 + c, lanes] = m_h[j * c:(j + 1) * c]
                l_s[t, p, sl, row0:row0 + c, lanes] = pv[src, LANES + hh * HEAD_DIM:LANES + (hh + 1) * HEAD_DIM]

    def combine(t, r):
        m0, m1, m2 = m_s[t, 0, r], m_s[t, 1, r], m_s[t, 2, r]
        m_all = jnp.maximum(jnp.maximum(m0, m1), m2)
        w0, w1, w2 = jnp.exp(m0 - m_all), jnp.exp(m1 - m_all), jnp.exp(m2 - m_all)
        num = w0 * acc_s[t, 0, r] + w1 * acc_s[t, 1, r] + w2 * acc_s[t, 2, r]
        den = w0 * l_s[t, 0, r] + w1 * l_s[t, 1, r] + w2 * l_s[t, 2, r]
        o_ref[t * w:(t + 1) * w, r * LANES:(r + 1) * LANES] = (num / den).astype(BF16)

    for t in range(TILES):
        load_slabs(t)
    for t in range(TILES):
        for p in reversed(range(npat)):
            for blk in _pattern_blocks(p):
                attend(t, p, blk)
        for r in range(MAX_DIL):
            combine(t, r)


def _attn_call(q, k, v, bias, batch, seq):
    rows, width = q.shape[1:]
    supers_per_seq = seq // SUPER
    assert supers_per_seq % TILES == 0
    w = WINDOW_STEPS
    tok = pl.BlockSpec((None, TILES * w, width), lambda hp, st: (hp, st, 0))
    npat = len(DILATIONS)
    slab = pltpu.VMEM((TILES, MAX_DIL, w, LANES), F32)
    part = pltpu.VMEM((TILES, npat, MAX_DIL, w, LANES), F32)
    chunks = pltpu.VMEM((TILES, npat, MAX_DIL, w, LANES), BF16)
    carry = pltpu.VMEM((npat, MAX_DIL, w, LANES), BF16)
    return pl.pallas_call(
        functools.partial(_attn_kernel, supers_per_seq=supers_per_seq),
        grid=(N_PAIRS, rows // (TILES * w)),
        in_specs=[tok, tok, tok,
                  pl.BlockSpec((npat, 2, None, 2 * w, 2 * w), lambda hp, st: (0, 0, hp, 0, 0))],
        out_specs=tok,
        out_shape=jax.ShapeDtypeStruct((N_PAIRS, rows, width), BF16),
        scratch_shapes=[slab, slab, slab, slab, chunks, chunks, carry, carry, part, part, part],
        compiler_params=pltpu.CompilerParams(dimension_semantics=("arbitrary", "arbitrary"),
                                             vmem_limit_bytes=VMEM_LIMIT),
        name="attn",
    )(q, k, v, bias)


def _mlp_kernel(x_ref, ypool_ref, yattn_ref, wo_ref, g_ref, wu_ref, wd_ref, o_ref, relay):
    tm = x_ref.shape[0]
    for hp in range(N_PAIRS):
        for r in range(MAX_DIL):
            relay[hp, pl.ds(r, tm // MAX_DIL, stride=MAX_DIL), :] = (
                yattn_ref[hp, :, r * LANES:(r + 1) * LANES].astype(F32))
    mixed = jnp.concatenate([ypool_ref[...]] + [relay[hp].astype(BF16) for hp in range(N_PAIRS)], axis=-1)
    h = x_ref[...] + jnp.dot(mixed, wo_ref[...], preferred_element_type=F32)
    c = _rms(h, g_ref[...]).astype(BF16)
    acc = h
    for f in range(0, D_FF, FF_CHUNK):
        up = jnp.dot(c, wu_ref[:, f:f + FF_CHUNK], preferred_element_type=F32)
        ff = jnp.square(jnp.maximum(up, 0.0)).astype(BF16)
        acc = acc + jnp.dot(ff, wd_ref[f:f + FF_CHUNK, :], preferred_element_type=F32)
    o_ref[...] = acc


def _mlp_call(x2, ypool, yattn, w_out, mlp_g, w_up, w_down):
    n = x2.shape[0]
    tm = TM_MLP
    const = lambda shape: pl.BlockSpec(shape, lambda i: (0,) * len(shape),
                                       pipeline_mode=pl.Buffered(1))
    return pl.pallas_call(
        _mlp_kernel,
        grid=(n // tm,),
        in_specs=[
            pl.BlockSpec((tm, D_MODEL), lambda i: (i, 0)),
            pl.BlockSpec((tm, POOL_WIDTH), lambda i: (i, 0)),
            pl.BlockSpec((N_PAIRS, tm // MAX_DIL, MAX_DIL * LANES), lambda i: (0, i, 0)),
            const((D_MODEL, D_MODEL)),
            const((1, D_MODEL)),
            const((D_MODEL, D_FF)),
            const((D_FF, D_MODEL)),
        ],
        out_specs=pl.BlockSpec((tm, D_MODEL), lambda i: (i, 0)),
        out_shape=jax.ShapeDtypeStruct((n, D_MODEL), F32),
        scratch_shapes=[pltpu.VMEM((N_PAIRS, tm, LANES), F32)],
        compiler_params=pltpu.CompilerParams(dimension_semantics=("arbitrary",),
                                             vmem_limit_bytes=VMEM_LIMIT),
        name="mlp",
    )(x2, ypool, yattn, w_out, mlp_g, w_up, w_down)


def kernel(x, mix_norm_g, w_in, pool_w, pool_scale, q_norm_g, k_norm_g, rel_bias,
           w_out, mlp_norm_g, w_up, w_down):
    batch, seq, d = x.shape
    assert d == D_MODEL and seq % SUPER == 0 and seq % TM_PROJ == 0
    x2 = x.reshape(batch * seq, d)
    tile_heads = lambda g: jnp.tile(g.astype(F32), N_HEADS).reshape(1, ATTN_WIDTH)

    ypool, q, k, v, w_out16, w_up16, w_down16 = _proj_call(
        x2, mix_norm_g.reshape(1, d), w_in,
        tile_heads(q_norm_g) * (HEAD_DIM ** -0.5), tile_heads(k_norm_g), pool_w.astype(BF16),
        pool_scale.reshape(1, POOL_WIDTH), (w_out, w_up, w_down), seq)
    bias = _bias_call(rel_bias)
    yattn = _attn_call(q, k, v, bias, batch, seq)
    y = _mlp_call(x2, ypool, yattn, w_out16, mlp_norm_g.reshape(1, d), w_up16, w_down16)
    return y.reshape(batch, seq, d)
```

```python
import functools
import math

import numpy as np
import jax
import jax.numpy as jnp
from jax import lax
from jax.experimental import pallas as pl
from jax.experimental.pallas import tpu as pltpu

D_MODEL = 1024
POOL_WIDTH = 512
POOL_WINDOWS = (2, 4, 8, 16)
POOL_GROUP_DIM = 128
ATTN_WIDTH = 512
HEAD_DIM = 64
N_HEADS = 8
DILATIONS = (1, 4, 16)
WINDOW_STEPS = 128
N_BUCKETS = 32
MAX_DISTANCE = 2048
D_FF = 4096
NORM_EPS = 1e-6
NEG_INF = -1e30

LANES = 128
N_PAIRS = ATTN_WIDTH // LANES
MAX_DIL = 16
SUPER = WINDOW_STEPS * MAX_DIL
HALO = 32
TM_PROJ = 1024
RELAY_PITCH = TM_PROJ // MAX_DIL + 8
TM_MLP = 1024
FF_CHUNK = 1024
VMEM_LIMIT = 56 * 1024 * 1024

F32 = jnp.float32
BF16 = jnp.bfloat16


def _rms(x, g):
    return x * lax.rsqrt(jnp.mean(x * x, axis=-1, keepdims=True) + NORM_EPS) * g


def _proj_kernel(x_ref, g_ref, w32_ref, qg_ref, kg_ref, pw_ref, ps_ref, wo32_ref, wu32_ref, wd32_ref,
                 ypool_ref, q_ref, k_ref, v_ref, wo_ref, wu_ref, wd_ref, w_ref, ubuf, sbuf, relay,
                 *, tiles_per_seq):
    i = pl.program_id(0)
    tm = x_ref.shape[0]

    @pl.when(i == 0)
    def _():
        w_ref[...] = w32_ref[...].astype(BF16)

    wo_ref[...] = wo32_ref[...].astype(BF16)
    wu_ref[...] = wu32_ref[...].astype(BF16)
    wd_ref[...] = wd32_ref[...].astype(BF16)

    seq_tile = i % tiles_per_seq

    @pl.when(seq_tile == 0)
    def _():
        ubuf[0:HALO, :] = jnp.zeros((HALO, POOL_WIDTH), F32)

    @pl.when(seq_tile != 0)
    def _():
        ubuf[0:HALO, :] = ubuf[tm:tm + HALO, :]

    a = _rms(x_ref[...], g_ref[...]).astype(BF16)

    def head_norm(t, gain):
        sq = t * t
        head0 = lax.broadcasted_iota(jnp.int32, (tm, LANES), 1) < HEAD_DIM
        zero = jnp.zeros((tm, LANES), F32)
        ssq = []
        for c in range(0, ATTN_WIDTH, LANES):
            s = sq[:, c:c + LANES]
            s0 = jnp.sum(jnp.where(head0, s, zero), axis=-1, keepdims=True)
            s1 = jnp.sum(jnp.where(head0, zero, s), axis=-1, keepdims=True)
            ssq.append(jnp.where(head0, s0, s1))
        ms = jnp.concatenate(ssq, axis=-1) * (1.0 / HEAD_DIM)
        return t * lax.rsqrt(ms + NORM_EPS) * gain

    def put_pairs(ref, which, t):
        rows = tm // MAX_DIL
        for hp in range(N_PAIRS):
            for g in range(tm // 8):
                j, r0 = g // 2, 8 * (g % 2)
                relay[which, hp, pl.ds(r0 * RELAY_PITCH + j, 8, stride=RELAY_PITCH), :] = (
                    t[8 * g:8 * g + 8, hp * LANES:(hp + 1) * LANES])
        for hp in range(N_PAIRS):
            for r in range(MAX_DIL):
                ref[hp, :, r * LANES:(r + 1) * LANES] = (
                    relay[which, hp, r * RELAY_PITCH:r * RELAY_PITCH + rows, :].astype(BF16))

    u = jnp.dot(a, w_ref[:, :POOL_WIDTH], preferred_element_type=F32)
    ubuf[HALO:HALO + tm, :] = u
    pos = seq_tile * tm + lax.broadcasted_iota(jnp.int32, (tm, 1), 0)
    pooled = []
    end = HALO + tm
    for g, w in enumerate(POOL_WINDOWS):
        cols = slice(g * POOL_GROUP_DIM, (g + 1) * POOL_GROUP_DIM)
        levels = int(math.log2(w))
        src, span = ubuf, 1
        for lvl in range(levels):
            lo = HALO if lvl == levels - 1 else HALO - 8 * (levels - 1 - lvl)
            summed = src[lo:end, cols] + src[lo - span:end - span, cols]
            span *= 2
            if lvl == levels - 1:
                wsum = summed
            else:
                sbuf[lvl, lo:end, cols] = summed
                src = sbuf.at[lvl]
        inv_count = 1.0 / jnp.minimum(pos + 1, w).astype(F32)
        pooled.append((wsum * inv_count - ubuf[HALO:end, cols]).astype(BF16))

    q = jnp.dot(a, w_ref[:, POOL_WIDTH:POOL_WIDTH + ATTN_WIDTH], preferred_element_type=F32)
    put_pairs(q_ref, 0, head_norm(q, qg_ref[...]))
    k = jnp.dot(a, w_ref[:, POOL_WIDTH + ATTN_WIDTH:POOL_WIDTH + 2 * ATTN_WIDTH],
                preferred_element_type=F32)
    put_pairs(k_ref, 1, head_norm(k, kg_ref[...]))
    v = jnp.dot(a, w_ref[:, POOL_WIDTH + 2 * ATTN_WIDTH:], preferred_element_type=F32)
    put_pairs(v_ref, 2, v)

    for g in range(len(POOL_WINDOWS)):
        cols = slice(g * POOL_GROUP_DIM, (g + 1) * POOL_GROUP_DIM)
        mixed = jnp.dot(pooled[g], pw_ref[g], preferred_element_type=F32)
        ypool_ref[:, cols] = (mixed * ps_ref[:, cols]).astype(BF16)


def _proj_call(x2, mix_g, w_in, qg, kg, pool_w, pool_scale, later_weights, seq):
    n = x2.shape[0]
    tm = TM_PROJ
    steps = n // tm
    const = lambda shape, **kw: pl.BlockSpec(shape, lambda i: (0,) * len(shape), **kw)
    row_slice = lambda wgt: pl.BlockSpec((wgt.shape[0] // steps, wgt.shape[1]), lambda i: (i, 0))
    pair_spec = pl.BlockSpec((N_PAIRS, tm // MAX_DIL, MAX_DIL * LANES), lambda i: (0, i, 0))
    pair_shape = jax.ShapeDtypeStruct((N_PAIRS, n // MAX_DIL, MAX_DIL * LANES), BF16)
    return pl.pallas_call(
        functools.partial(_proj_kernel, tiles_per_seq=seq // tm),
        grid=(n // tm,),
        in_specs=[
            pl.BlockSpec((tm, D_MODEL), lambda i: (i, 0)),
            const((1, D_MODEL)),
            const((D_MODEL, POOL_WIDTH + 3 * ATTN_WIDTH), pipeline_mode=pl.Buffered(1)),
            const((1, ATTN_WIDTH)),
            const((1, ATTN_WIDTH)),
            const((len(POOL_WINDOWS), POOL_GROUP_DIM, POOL_GROUP_DIM)),
            const((1, POOL_WIDTH)),
        ] + [row_slice(wgt) for wgt in later_weights],
        out_specs=[pl.BlockSpec((tm, POOL_WIDTH), lambda i: (i, 0)), pair_spec, pair_spec, pair_spec]
        + [row_slice(wgt) for wgt in later_weights],
        out_shape=[jax.ShapeDtypeStruct((n, POOL_WIDTH), BF16), pair_shape, pair_shape, pair_shape]
        + [jax.ShapeDtypeStruct(wgt.shape, BF16) for wgt in later_weights],
        scratch_shapes=[pltpu.VMEM((D_MODEL, POOL_WIDTH + 3 * ATTN_WIDTH), BF16),
                        pltpu.VMEM((HALO + tm, POOL_WIDTH), F32),
                        pltpu.VMEM((3, HALO + tm, POOL_WIDTH), F32),
                        pltpu.VMEM((3, N_PAIRS, MAX_DIL * RELAY_PITCH, LANES), F32)],
        compiler_params=pltpu.CompilerParams(dimension_semantics=("arbitrary",),
                                             vmem_limit_bytes=VMEM_LIMIT),
        name="proj",
    )(x2, mix_g, w_in, qg, kg, pool_w, pool_scale, *later_weights)


def _block_orders():
    out = []
    for dil in DILATIONS:
        ns = MAX_DIL // dil
        c = WINDOW_STEPS // ns
        slab = np.arange(ns)[:, None]
        q_step = (np.arange(c)[None, :] * ns + slab + WINDOW_STEPS).reshape(-1)
        k_step = np.concatenate([q_step - WINDOW_STEPS, q_step])
        out.append((dil, ns, c, q_step, k_step))
    return out


def _bucket_tables():
    max_exact = N_BUCKETS // 2
    buckets, prev_cols = [], []
    for dil, ns, c, q_step, k_step in _block_orders():
        dist = q_step[:, None] - k_step[None, :]
        ok = (dist >= 0) & (dist <= WINDOW_STEPS)
        tok = np.clip(dist, 0, WINDOW_STEPS) * dil
        d_f = np.maximum(tok, 1).astype(np.float32)
        large = max_exact + (np.log(d_f / np.float32(max_exact)) / np.float32(math.log(MAX_DISTANCE / max_exact))
                             * np.float32(N_BUCKETS - max_exact)).astype(np.int32)
        large = np.minimum(large, N_BUCKETS - 1)
        bucket = np.where(tok < max_exact, tok, large)
        buckets.append(np.where(ok, bucket, -1).astype(np.int32))
        prev_cols.append(np.broadcast_to((k_step < WINDOW_STEPS)[None, :], dist.shape).astype(np.int32))
    return np.stack(buckets), np.stack(prev_cols)


def _bias_kernel(rel_ref, bucket_ref, prev_ref, out_ref):
    bucket = bucket_ref[0]
    is_prev = prev_ref[0] != 0
    for h in range(N_HEADS):
        tab = jnp.full(bucket.shape, NEG_INF, F32)
        for b in range(N_BUCKETS):
            tab = jnp.where(bucket == b, rel_ref[b, h], tab)
        rows = slice((h % 2) * WINDOW_STEPS, (h % 2 + 1) * WINDOW_STEPS)
        out_ref[0, 0, h // 2, rows, :] = tab
        out_ref[0, 1, h // 2, rows, :] = jnp.where(is_prev, NEG_INF, tab)


def _bias_call(rel_bias):
    bucket, prev_cols = _bucket_tables()
    npat = len(DILATIONS)
    tab_spec = pl.BlockSpec((1, WINDOW_STEPS, 2 * WINDOW_STEPS), lambda p: (p, 0, 0))
    return pl.pallas_call(
        _bias_kernel,
        grid=(npat,),
        in_specs=[pl.BlockSpec(memory_space=pltpu.SMEM), tab_spec, tab_spec],
        out_specs=pl.BlockSpec((1, 2, N_PAIRS, 2 * WINDOW_STEPS, 2 * WINDOW_STEPS),
                               lambda p: (p, 0, 0, 0, 0)),
        out_shape=jax.ShapeDtypeStruct((npat, 2, N_PAIRS, 2 * WINDOW_STEPS, 2 * WINDOW_STEPS), F32),
        name="bias",
    )(rel_bias, jnp.asarray(bucket), jnp.asarray(prev_cols))


TILES = 2


def _pattern_blocks(p):
    dil = DILATIONS[p]
    ns = MAX_DIL // dil
    c = WINDOW_STEPS // ns
    return [(a * ns + n, [a + dil * j for j in range(ns)], c * n, c, a, n)
            for a in range(dil) for n in range(ns)]


def _attn_kernel(q_ref, k_ref, v_ref, bias_ref, o_ref,
                 qa32, qb32, k32, v32, kc_s, vc_s, kcarry, vcarry, acc_s, m_s, l_s, *, supers_per_seq):
    step = pl.program_id(1)
    is_first = (step * TILES) % supers_per_seq == 0
    first = jnp.where(is_first, 1, 0)
    w = WINDOW_STEPS
    npat = len(DILATIONS)

    @pl.when(is_first)
    def _():
        for p in range(npat):
            for a in range(DILATIONS[p]):
                kcarry[p, a] = jnp.zeros((w, LANES), BF16)
                vcarry[p, a] = jnp.zeros((w, LANES), BF16)

    @pl.when(jnp.logical_not(is_first))
    def _():
        for p in range(npat):
            ns = MAX_DIL // DILATIONS[p]
            for a in range(DILATIONS[p]):
                kcarry[p, a] = kc_s[TILES - 1, p, a * ns + ns - 1]
                vcarry[p, a] = vc_s[TILES - 1, p, a * ns + ns - 1]

    head0 = lax.broadcasted_iota(jnp.int32, (w, LANES), 1) < HEAD_DIM

    def load_slabs(t):
        rows = slice(t * w, (t + 1) * w)
        for r in range(MAX_DIL):
            cols = slice(r * LANES, (r + 1) * LANES)
            q = q_ref[rows, cols].astype(F32)
            zero = jnp.zeros_like(q)
            qa32[t, r] = jnp.where(head0, q, zero)
            qb32[t, r] = jnp.where(head0, zero, q)
            k32[t, r] = k_ref[rows, cols].astype(F32)
            v32[t, r] = v_ref[rows, cols].astype(F32)

    def gather(ref, t, slabs, row0, c):
        return jnp.concatenate([ref[t, s, row0:row0 + c, :] for s in slabs], axis=0)

    def attend(t, p, blk):
        b, slabs, row0, c, stream, n = blk
        ns = MAX_DIL // DILATIONS[p]
        k_cur = gather(k32, t, slabs, row0, c).astype(BF16)
        v_cur = gather(v32, t, slabs, row0, c).astype(BF16)
        qm = jnp.concatenate([gather(qa32, t, slabs, row0, c), gather(qb32, t, slabs, row0, c)],
                             axis=0).astype(BF16)
        if n:
            k_prev, v_prev = kc_s[t, p, b - 1], vc_s[t, p, b - 1]
        elif t:
            k_prev, v_prev = kc_s[t - 1, p, b + ns - 1], vc_s[t - 1, p, b + ns - 1]
        else:
            k_prev, v_prev = kcarry[p, stream], vcarry[p, stream]
        kk = jnp.concatenate([k_prev, k_cur], axis=0)
        vv = jnp.concatenate([v_prev, v_cur], axis=0)
        kc_s[t, p, b] = k_cur
        vc_s[t, p, b] = v_cur
        bias = bias_ref[p, first] if (n == 0 and t == 0) else bias_ref[p, 0]
        s = lax.dot_general(qm, kk, (((1,), (1,)), ((), ())), preferred_element_type=F32) + bias
        m = jnp.max(s, axis=-1, keepdims=True)
        e = jnp.exp(s - m).astype(BF16)
        pv = jnp.dot(e, jnp.concatenate([vv, jnp.ones_like(vv)], axis=1), preferred_element_type=F32)
        for hh in range(2):
            lanes = slice(hh * HEAD_DIM, (hh + 1) * HEAD_DIM)
            m_h = jnp.broadcast_to(m[hh * w:(hh + 1) * w], (w, HEAD_DIM))
            for j, sl in enumerate(slabs):
                src = slice(hh * w + j * c, hh * w + (j + 1) * c)
                acc_s[t, p, sl, row0:row0 + c, lanes] = pv[src, lanes]
                m_s[t, p, sl, row0:row0 + c, lanes] = m_h[j * c:(j + 1) * c]
                l_s[t, p, sl, row0:row0 + c, lanes] = pv[src, LANES + hh * HEAD_DIM:LANES + (hh + 1) * HEAD_DIM]

    def combine(t, r):
        m0, m1, m2 = m_s[t, 0, r], m_s[t, 1, r], m_s[t, 2, r]
        m_all = jnp.maximum(jnp.maximum(m0, m1), m2)
        w0, w1, w2 = jnp.exp(m0 - m_all), jnp.exp(m1 - m_all), jnp.exp(m2 - m_all)
        num = w0 * acc_s[t, 0, r] + w1 * acc_s[t, 1, r] + w2 * acc_s[t, 2, r]
        den = w0 * l_s[t, 0, r] + w1 * l_s[t, 1, r] + w2 * l_s[t, 2, r]
        o_ref[t * w:(t + 1) * w, r * LANES:(r + 1) * LANES] = (num / den).astype(BF16)

    for t in range(TILES):
        load_slabs(t)
    for t in range(TILES):
        for p in reversed(range(npat)):
            for blk in _pattern_blocks(p):
                attend(t, p, blk)
        for r in range(MAX_DIL):
            combine(t, r)


def _attn_call(q, k, v, bias, seq):
    rows, width = q.shape[1:]
    supers_per_seq = seq // SUPER
    assert supers_per_seq % TILES == 0
    w = WINDOW_STEPS
    tok = pl.BlockSpec((None, TILES * w, width), lambda hp, st: (hp, st, 0))
    npat = len(DILATIONS)
    slab = pltpu.VMEM((TILES, MAX_DIL, w, LANES), F32)
    part = pltpu.VMEM((TILES, npat, MAX_DIL, w, LANES), F32)
    chunks = pltpu.VMEM((TILES, npat, MAX_DIL, w, LANES), BF16)
    carry = pltpu.VMEM((npat, MAX_DIL, w, LANES), BF16)
    return pl.pallas_call(
        functools.partial(_attn_kernel, supers_per_seq=supers_per_seq),
        grid=(N_PAIRS, rows // (TILES * w)),
        in_specs=[tok, tok, tok,
                  pl.BlockSpec((npat, 2, None, 2 * w, 2 * w), lambda hp, st: (0, 0, hp, 0, 0))],
        out_specs=tok,
        out_shape=jax.ShapeDtypeStruct((N_PAIRS, rows, width), BF16),
        scratch_shapes=[slab, slab, slab, slab, chunks, chunks, carry, carry, part, part, part],
        compiler_params=pltpu.CompilerParams(dimension_semantics=("arbitrary", "arbitrary"),
                                             vmem_limit_bytes=VMEM_LIMIT),
        name="attn",
    )(q, k, v, bias)


def _mlp_kernel(x_ref, ypool_ref, yattn_ref, wo_ref, g_ref, wu_ref, wd_ref, o_ref, relay):
    tm = x_ref.shape[0]
    for hp in range(N_PAIRS):
        for r in range(MAX_DIL):
            relay[hp, pl.ds(r, tm // MAX_DIL, stride=MAX_DIL), :] = (
                yattn_ref[hp, :, r * LANES:(r + 1) * LANES].astype(F32))
    mixed = jnp.concatenate([ypool_ref[...]] + [relay[hp].astype(BF16) for hp in range(N_PAIRS)], axis=-1)
    h = x_ref[...] + jnp.dot(mixed, wo_ref[...], preferred_element_type=F32)
    c = _rms(h, g_ref[...]).astype(BF16)
    acc = h
    for f in range(0, D_FF, FF_CHUNK):
        up = jnp.dot(c, wu_ref[:, f:f + FF_CHUNK], preferred_element_type=F32)
        ff = jnp.square(jnp.maximum(up, 0.0)).astype(BF16)
        acc = acc + jnp.dot(ff, wd_ref[f:f + FF_CHUNK, :], preferred_element_type=F32)
    o_ref[...] = acc


def _mlp_call(x2, ypool, yattn, w_out, mlp_g, w_up, w_down):
    n = x2.shape[0]
    tm = TM_MLP
    const = lambda shape: pl.BlockSpec(shape, lambda i: (0,) * len(shape),
                                       pipeline_mode=pl.Buffered(1))
    return pl.pallas_call(
        _mlp_kernel,
        grid=(n // tm,),
        in_specs=[
            pl.BlockSpec((tm, D_MODEL), lambda i: (i, 0)),
            pl.BlockSpec((tm, POOL_WIDTH), lambda i: (i, 0)),
            pl.BlockSpec((N_PAIRS, tm // MAX_DIL, MAX_DIL * LANES), lambda i: (0, i, 0)),
            const((D_MODEL, D_MODEL)),
            const((1, D_MODEL)),
            const((D_MODEL, D_FF)),
            const((D_FF, D_MODEL)),
        ],
        out_specs=pl.BlockSpec((tm, D_MODEL), lambda i: (i, 0)),
        out_shape=jax.ShapeDtypeStruct((n, D_MODEL), F32),
        scratch_shapes=[pltpu.VMEM((N_PAIRS, tm, LANES), F32)],
        compiler_params=pltpu.CompilerParams(dimension_semantics=("arbitrary",),
                                             vmem_limit_bytes=VMEM_LIMIT),
        name="mlp",
    )(x2, ypool, yattn, w_out, mlp_g, w_up, w_down)


def kernel(x, mix_norm_g, w_in, pool_w, pool_scale, q_norm_g, k_norm_g, rel_bias,
           w_out, mlp_norm_g, w_up, w_down):
    batch, seq, d = x.shape
    assert d == D_MODEL and seq % SUPER == 0 and seq % TM_PROJ == 0
    x2 = x.reshape(batch * seq, d)
    tile_heads = lambda g: jnp.tile(g.astype(F32), N_HEADS).reshape(1, ATTN_WIDTH)

    ypool, q, k, v, w_out16, w_up16, w_down16 = _proj_call(
        x2, mix_norm_g.reshape(1, d), w_in,
        tile_heads(q_norm_g) * (HEAD_DIM ** -0.5), tile_heads(k_norm_g), pool_w.astype(BF16),
        pool_scale.reshape(1, POOL_WIDTH), (w_out, w_up, w_down), seq)
    bias = _bias_call(rel_bias)
    yattn = _attn_call(q, k, v, bias, seq)
    y = _mlp_call(x2, ypool, yattn, w_out16, mlp_norm_g.reshape(1, d), w_up16, w_down16)
    return y.reshape(batch, seq, d)
```

```python
import functools
import math

import numpy as np
import jax
import jax.numpy as jnp
from jax import lax
from jax.experimental import pallas as pl
from jax.experimental.pallas import tpu as pltpu

D_MODEL = 1024
POOL_WIDTH = 512
POOL_WINDOWS = (2, 4, 8, 16)
POOL_GROUP_DIM = 128
ATTN_WIDTH = 512
HEAD_DIM = 64
N_HEADS = 8
DILATIONS = (1, 4, 16)
WINDOW_STEPS = 128
N_BUCKETS = 32
MAX_DISTANCE = 2048
D_FF = 4096
NORM_EPS = 1e-6
NEG_INF = -1e30

LANES = 128
N_PAIRS = ATTN_WIDTH // LANES
MAX_DIL = 16
SUPER = WINDOW_STEPS * MAX_DIL
HALO = 32
TM_PROJ = 1024
RELAY_PITCH = TM_PROJ // MAX_DIL + 8
SUM_SLOTS = [(lvl, g) for g, wnd in enumerate(POOL_WINDOWS) for lvl in range(int(math.log2(wnd)) - 1)]
WEIGHT_STEPS = 16
TM_MLP = 1024
FF_CHUNK = 1024
VMEM_LIMIT = 56 * 1024 * 1024

F32 = jnp.float32
BF16 = jnp.bfloat16


def _rms(x, g):
    return x * lax.rsqrt(jnp.mean(x * x, axis=-1, keepdims=True) + NORM_EPS) * g


def _proj_kernel(x_ref, g_ref, w32_ref, qg_ref, kg_ref, pw_ref, ps_ref, wo32_ref, wu32_ref, wd32_ref,
                 ypool_ref, q_ref, k_ref, v_ref, wo_ref, wu_ref, wd_ref, w_ref, ubuf, sbuf, relay,
                 *, tiles_per_seq):
    i = pl.program_id(0)
    tm = x_ref.shape[0]

    @pl.when(i == 0)
    def _():
        w_ref[...] = w32_ref[...].astype(BF16)

    wo_ref[...] = wo32_ref[...].astype(BF16)
    wu_ref[...] = wu32_ref[...].astype(BF16)
    wd_ref[...] = wd32_ref[...].astype(BF16)

    seq_tile = i % tiles_per_seq

    @pl.when(seq_tile == 0)
    def _():
        ubuf[0:HALO, :] = jnp.zeros((HALO, POOL_WIDTH), F32)

    @pl.when(seq_tile != 0)
    def _():
        ubuf[0:HALO, :] = ubuf[tm:tm + HALO, :]

    a = _rms(x_ref[...], g_ref[...]).astype(BF16)

    def head_norm(t, gain):
        sq = t * t
        head0 = lax.broadcasted_iota(jnp.int32, (tm, LANES), 1) < HEAD_DIM
        zero = jnp.zeros((tm, LANES), F32)
        ssq = []
        for c in range(0, ATTN_WIDTH, LANES):
            s = sq[:, c:c + LANES]
            s0 = jnp.sum(jnp.where(head0, s, zero), axis=-1, keepdims=True)
            s1 = jnp.sum(jnp.where(head0, zero, s), axis=-1, keepdims=True)
            ssq.append(jnp.where(head0, s0, s1))
        ms = jnp.concatenate(ssq, axis=-1) * (1.0 / HEAD_DIM)
        return t * lax.rsqrt(ms + NORM_EPS) * gain

    def put_pairs(ref, which, t):
        rows = tm // MAX_DIL
        for hp in range(N_PAIRS):
            for g in range(tm // 8):
                j, r0 = g // 2, 8 * (g % 2)
                relay[which, hp, pl.ds(r0 * RELAY_PITCH + j, 8, stride=RELAY_PITCH), :] = (
                    t[8 * g:8 * g + 8, hp * LANES:(hp + 1) * LANES])
        for hp in range(N_PAIRS):
            for r in range(MAX_DIL):
                ref[hp, :, r * LANES:(r + 1) * LANES] = (
                    relay[which, hp, r * RELAY_PITCH:r * RELAY_PITCH + rows, :].astype(BF16))

    u = jnp.dot(a, w_ref[:, :POOL_WIDTH], preferred_element_type=F32)
    ubuf[HALO:HALO + tm, :] = u
    pos = seq_tile * tm + lax.broadcasted_iota(jnp.int32, (tm, 1), 0)
    pooled = []
    end = HALO + tm
    for g, w in enumerate(POOL_WINDOWS):
        cols = slice(g * POOL_GROUP_DIM, (g + 1) * POOL_GROUP_DIM)
        levels = int(math.log2(w))
        src, src_cols, span = ubuf, cols, 1
        for lvl in range(levels):
            lo = HALO if lvl == levels - 1 else HALO - 8 * (levels - 1 - lvl)
            summed = src[lo:end, src_cols] + src[lo - span:end - span, src_cols]
            span *= 2
            if lvl == levels - 1:
                wsum = summed
            else:
                slot = SUM_SLOTS.index((lvl, g))
                sbuf[slot, lo:end, :] = summed
                src, src_cols = sbuf.at[slot], slice(None)
        inv_count = 1.0 / jnp.minimum(pos + 1, w).astype(F32)
        pooled.append((wsum * inv_count - ubuf[HALO:end, cols]).astype(BF16))

    q = jnp.dot(a, w_ref[:, POOL_WIDTH:POOL_WIDTH + ATTN_WIDTH], preferred_element_type=F32)
    put_pairs(q_ref, 0, head_norm(q, qg_ref[...]))
    k = jnp.dot(a, w_ref[:, POOL_WIDTH + ATTN_WIDTH:POOL_WIDTH + 2 * ATTN_WIDTH],
                preferred_element_type=F32)
    put_pairs(k_ref, 1, head_norm(k, kg_ref[...]))
    v = jnp.dot(a, w_ref[:, POOL_WIDTH + 2 * ATTN_WIDTH:], preferred_element_type=F32)
    put_pairs(v_ref, 2, v)

    for g in range(len(POOL_WINDOWS)):
        cols = slice(g * POOL_GROUP_DIM, (g + 1) * POOL_GROUP_DIM)
        mixed = jnp.dot(pooled[g], pw_ref[g], preferred_element_type=F32)
        ypool_ref[:, cols] = (mixed * ps_ref[:, cols]).astype(BF16)


def _proj_call(x2, mix_g, w_in, qg, kg, pool_w, pool_scale, later_weights, seq):
    n = x2.shape[0]
    tm = TM_PROJ
    const = lambda shape, **kw: pl.BlockSpec(shape, lambda i: (0,) * len(shape), **kw)
    row_slice = lambda wgt: pl.BlockSpec((wgt.shape[0] // WEIGHT_STEPS, wgt.shape[1]),
                                         lambda i: (jnp.minimum(i, WEIGHT_STEPS - 1), 0))
    pair_spec = pl.BlockSpec((N_PAIRS, tm // MAX_DIL, MAX_DIL * LANES), lambda i: (0, i, 0))
    pair_shape = jax.ShapeDtypeStruct((N_PAIRS, n // MAX_DIL, MAX_DIL * LANES), BF16)
    return pl.pallas_call(
        functools.partial(_proj_kernel, tiles_per_seq=seq // tm),
        grid=(n // tm,),
        in_specs=[
            pl.BlockSpec((tm, D_MODEL), lambda i: (i, 0)),
            const((1, D_MODEL)),
            const((D_MODEL, POOL_WIDTH + 3 * ATTN_WIDTH), pipeline_mode=pl.Buffered(1)),
            const((1, ATTN_WIDTH)),
            const((1, ATTN_WIDTH)),
            const((len(POOL_WINDOWS), POOL_GROUP_DIM, POOL_GROUP_DIM)),
            const((1, POOL_WIDTH)),
        ] + [row_slice(wgt) for wgt in later_weights],
        out_specs=[pl.BlockSpec((tm, POOL_WIDTH), lambda i: (i, 0)), pair_spec, pair_spec, pair_spec]
        + [row_slice(wgt) for wgt in later_weights],
        out_shape=[jax.ShapeDtypeStruct((n, POOL_WIDTH), BF16), pair_shape, pair_shape, pair_shape]
        + [jax.ShapeDtypeStruct(wgt.shape, BF16) for wgt in later_weights],
        scratch_shapes=[pltpu.VMEM((D_MODEL, POOL_WIDTH + 3 * ATTN_WIDTH), BF16),
                        pltpu.VMEM((HALO + tm, POOL_WIDTH), F32),
                        pltpu.VMEM((len(SUM_SLOTS), HALO + tm, POOL_GROUP_DIM), F32),
                        pltpu.VMEM((3, N_PAIRS, MAX_DIL * RELAY_PITCH, LANES), F32)],
        compiler_params=pltpu.CompilerParams(dimension_semantics=("arbitrary",),
                                             vmem_limit_bytes=VMEM_LIMIT),
        name="proj",
    )(x2, mix_g, w_in, qg, kg, pool_w, pool_scale, *later_weights)


def _block_orders():
    out = []
    for dil in DILATIONS:
        ns = MAX_DIL // dil
        c = WINDOW_STEPS // ns
        slab = np.arange(ns)[:, None]
        q_step = (np.arange(c)[None, :] * ns + slab + WINDOW_STEPS).reshape(-1)
        k_step = np.concatenate([q_step - WINDOW_STEPS, q_step])
        out.append((dil, ns, c, q_step, k_step))
    return out


def _bucket_tables():
    max_exact = N_BUCKETS // 2
    buckets, prev_cols = [], []
    for dil, ns, c, q_step, k_step in _block_orders():
        dist = q_step[:, None] - k_step[None, :]
        ok = (dist >= 0) & (dist <= WINDOW_STEPS)
        tok = np.clip(dist, 0, WINDOW_STEPS) * dil
        d_f = np.maximum(tok, 1).astype(np.float32)
        large = max_exact + (np.log(d_f / np.float32(max_exact)) / np.float32(math.log(MAX_DISTANCE / max_exact))
                             * np.float32(N_BUCKETS - max_exact)).astype(np.int32)
        large = np.minimum(large, N_BUCKETS - 1)
        bucket = np.where(tok < max_exact, tok, large)
        buckets.append(np.where(ok, bucket, -1).astype(np.int32))
        prev_cols.append(np.broadcast_to((k_step < WINDOW_STEPS)[None, :], dist.shape).astype(np.int32))
    return np.stack(buckets), np.stack(prev_cols)


def _bias_kernel(rel_ref, bucket_ref, prev_ref, out_ref):
    bucket = bucket_ref[0]
    is_prev = prev_ref[0] != 0
    for h in range(N_HEADS):
        tab = jnp.full(bucket.shape, NEG_INF, F32)
        for b in range(N_BUCKETS):
            tab = jnp.where(bucket == b, rel_ref[b, h], tab)
        rows = slice((h % 2) * WINDOW_STEPS, (h % 2 + 1) * WINDOW_STEPS)
        out_ref[0, 0, h // 2, rows, :] = tab
        out_ref[0, 1, h // 2, rows, :] = jnp.where(is_prev, NEG_INF, tab)


def _bias_call(rel_bias):
    bucket, prev_cols = _bucket_tables()
    npat = len(DILATIONS)
    tab_spec = pl.BlockSpec((1, WINDOW_STEPS, 2 * WINDOW_STEPS), lambda p: (p, 0, 0))
    return pl.pallas_call(
        _bias_kernel,
        grid=(npat,),
        in_specs=[pl.BlockSpec(memory_space=pltpu.SMEM), tab_spec, tab_spec],
        out_specs=pl.BlockSpec((1, 2, N_PAIRS, 2 * WINDOW_STEPS, 2 * WINDOW_STEPS),
                               lambda p: (p, 0, 0, 0, 0)),
        out_shape=jax.ShapeDtypeStruct((npat, 2, N_PAIRS, 2 * WINDOW_STEPS, 2 * WINDOW_STEPS), F32),
        name="bias",
    )(rel_bias, jnp.asarray(bucket), jnp.asarray(prev_cols))


TILES = 2


def _pattern_blocks(p):
    dil = DILATIONS[p]
    ns = MAX_DIL // dil
    c = WINDOW_STEPS // ns
    return [(a * ns + n, [a + dil * j for j in range(ns)], c * n, c, a, n)
            for a in range(dil) for n in range(ns)]


def _attn_kernel(q_ref, k_ref, v_ref, bias_ref, o_ref,
                 qa32, qb32, k32, v32, kc_s, vc_s, kcarry, vcarry, acc_s, m_s, l_s, *, supers_per_seq):
    step = pl.program_id(1)
    is_first = (step * TILES) % supers_per_seq == 0
    first = jnp.where(is_first, 1, 0)
    w = WINDOW_STEPS
    npat = len(DILATIONS)

    @pl.when(is_first)
    def _():
        for p in range(npat):
            for a in range(DILATIONS[p]):
                kcarry[p, a] = jnp.zeros((w, LANES), BF16)
                vcarry[p, a] = jnp.zeros((w, LANES), BF16)

    @pl.when(jnp.logical_not(is_first))
    def _():
        for p in range(npat):
            ns = MAX_DIL // DILATIONS[p]
            for a in range(DILATIONS[p]):
                kcarry[p, a] = kc_s[TILES - 1, p, a * ns + ns - 1]
                vcarry[p, a] = vc_s[TILES - 1, p, a * ns + ns - 1]

    head0 = lax.broadcasted_iota(jnp.int32, (w, LANES), 1) < HEAD_DIM

    def load_slabs(t):
        rows = slice(t * w, (t + 1) * w)
        for r in range(MAX_DIL):
            cols = slice(r * LANES, (r + 1) * LANES)
            q = q_ref[rows, cols].astype(F32)
            zero = jnp.zeros_like(q)
            qa32[t, r] = jnp.where(head0, q, zero)
            qb32[t, r] = jnp.where(head0, zero, q)
            k32[t, r] = k_ref[rows, cols].astype(F32)
            v32[t, r] = v_ref[rows, cols].astype(F32)

    def gather(ref, t, slabs, row0, c):
        return jnp.concatenate([ref[t, s, row0:row0 + c, :] for s in slabs], axis=0)

    def attend(t, p, blk):
        b, slabs, row0, c, stream, n = blk
        ns = MAX_DIL // DILATIONS[p]
        k_cur = gather(k32, t, slabs, row0, c).astype(BF16)
        v_cur = gather(v32, t, slabs, row0, c).astype(BF16)
        qm = jnp.concatenate([gather(qa32, t, slabs, row0, c), gather(qb32, t, slabs, row0, c)],
                             axis=0).astype(BF16)
        if n:
            k_prev, v_prev = kc_s[t, p, b - 1], vc_s[t, p, b - 1]
        elif t:
            k_prev, v_prev = kc_s[t - 1, p, b + ns - 1], vc_s[t - 1, p, b + ns - 1]
        else:
            k_prev, v_prev = kcarry[p, stream], vcarry[p, stream]
        kk = jnp.concatenate([k_prev, k_cur], axis=0)
        vv = jnp.concatenate([v_prev, v_cur], axis=0)
        kc_s[t, p, b] = k_cur
        vc_s[t, p, b] = v_cur
        bias = bias_ref[p, first] if (n == 0 and t == 0) else bias_ref[p, 0]
        s = lax.dot_general(qm, kk, (((1,), (1,)), ((), ())), preferred_element_type=F32) + bias
        m = jnp.max(s, axis=-1, keepdims=True)
        e = jnp.exp(s - m).astype(BF16)
        pv = jnp.dot(e, jnp.concatenate([vv, jnp.ones_like(vv)], axis=1), preferred_element_type=F32)
        for hh in range(2):
            lanes = slice(hh * HEAD_DIM, (hh + 1) * HEAD_DIM)
            m_h = jnp.broadcast_to(m[hh * w:(hh + 1) * w], (w, HEAD_DIM))
            for j, sl in enumerate(slabs):
                src = slice(hh * w + j * c, hh * w + (j + 1) * c)
                acc_s[t, p, sl, row0:row0 + c, lanes] = pv[src, lanes]
                m_s[t, p, sl, row0:row0 + c, lanes] = m_h[j * c:(j + 1) * c]
                l_s[t, p, sl, row0:row0 + c, lanes] = pv[src, LANES + hh * HEAD_DIM:LANES + (hh + 1) * HEAD_DIM]

    def combine(t, r):
        m0, m1, m2 = m_s[t, 0, r], m_s[t, 1, r], m_s[t, 2, r]
        m_all = jnp.maximum(jnp.maximum(m0, m1), m2)
        w0, w1, w2 = jnp.exp(m0 - m_all), jnp.exp(m1 - m_all), jnp.exp(m2 - m_all)
        num = w0 * acc_s[t, 0, r] + w1 * acc_s[t, 1, r] + w2 * acc_s[t, 2, r]
        den = w0 * l_s[t, 0, r] + w1 * l_s[t, 1, r] + w2 * l_s[t, 2, r]
        o_ref[t * w:(t + 1) * w, r * LANES:(r + 1) * LANES] = (num / den).astype(BF16)

    for t in range(TILES):
        load_slabs(t)
    for t in range(TILES):
        for p in reversed(range(npat)):
            for blk in _pattern_blocks(p):
                attend(t, p, blk)
        for r in range(MAX_DIL):
            combine(t, r)


def _attn_call(q, k, v, bias, seq):
    rows, width = q.shape[1:]
    supers_per_seq = seq // SUPER
    assert supers_per_seq % TILES == 0
    w = WINDOW_STEPS
    tok = pl.BlockSpec((None, TILES * w, width), lambda hp, st: (hp, st, 0))
    npat = len(DILATIONS)
    slab = pltpu.VMEM((TILES, MAX_DIL, w, LANES), F32)
    part = pltpu.VMEM((TILES, npat, MAX_DIL, w, LANES), F32)
    chunks = pltpu.VMEM((TILES, npat, MAX_DIL, w, LANES), BF16)
    carry = pltpu.VMEM((npat, MAX_DIL, w, LANES), BF16)
    return pl.pallas_call(
        functools.partial(_attn_kernel, supers_per_seq=supers_per_seq),
        grid=(N_PAIRS, rows // (TILES * w)),
        in_specs=[tok, tok, tok,
                  pl.BlockSpec((npat, 2, None, 2 * w, 2 * w), lambda hp, st: (0, 0, hp, 0, 0))],
        out_specs=tok,
        out_shape=jax.ShapeDtypeStruct((N_PAIRS, rows, width), BF16),
        scratch_shapes=[slab, slab, slab, slab, chunks, chunks, carry, carry, part, part, part],
        compiler_params=pltpu.CompilerParams(dimension_semantics=("arbitrary", "arbitrary"),
                                             vmem_limit_bytes=VMEM_LIMIT),
        name="attn",
    )(q, k, v, bias)


def _mlp_kernel(x_ref, ypool_ref, yattn_ref, wo_ref, g_ref, wu_ref, wd_ref, o_ref, relay):
    tm = x_ref.shape[0]
    for hp in range(N_PAIRS):
        for r in range(MAX_DIL):
            relay[hp, pl.ds(r, tm // MAX_DIL, stride=MAX_DIL), :] = (
                yattn_ref[hp, :, r * LANES:(r + 1) * LANES].astype(F32))
    mixed = jnp.concatenate([ypool_ref[...]] + [relay[hp].astype(BF16) for hp in range(N_PAIRS)], axis=-1)
    h = x_ref[...] + jnp.dot(mixed, wo_ref[...], preferred_element_type=F32)
    c = _rms(h, g_ref[...]).astype(BF16)
    acc = h
    for f in range(0, D_FF, FF_CHUNK):
        up = jnp.dot(c, wu_ref[:, f:f + FF_CHUNK], preferred_element_type=F32)
        ff = jnp.square(jnp.maximum(up, 0.0)).astype(BF16)
        acc = acc + jnp.dot(ff, wd_ref[f:f + FF_CHUNK, :], preferred_element_type=F32)
    o_ref[...] = acc


def _mlp_call(x2, ypool, yattn, w_out, mlp_g, w_up, w_down):
    n = x2.shape[0]
    tm = TM_MLP
    const = lambda shape: pl.BlockSpec(shape, lambda i: (0,) * len(shape),
                                       pipeline_mode=pl.Buffered(1))
    return pl.pallas_call(
        _mlp_kernel,
        grid=(n // tm,),
        in_specs=[
            pl.BlockSpec((tm, D_MODEL), lambda i: (i, 0)),
            pl.BlockSpec((tm, POOL_WIDTH), lambda i: (i, 0)),
            pl.BlockSpec((N_PAIRS, tm // MAX_DIL, MAX_DIL * LANES), lambda i: (0, i, 0)),
            const((D_MODEL, D_MODEL)),
            const((1, D_MODEL)),
            const((D_MODEL, D_FF)),
            const((D_FF, D_MODEL)),
        ],
        out_specs=pl.BlockSpec((tm, D_MODEL), lambda i: (i, 0)),
        out_shape=jax.ShapeDtypeStruct((n, D_MODEL), F32),
        scratch_shapes=[pltpu.VMEM((N_PAIRS, tm, LANES), F32)],
        compiler_params=pltpu.CompilerParams(dimension_semantics=("arbitrary",),
                                             vmem_limit_bytes=VMEM_LIMIT),
        name="mlp",
    )(x2, ypool, yattn, w_out, mlp_g, w_up, w_down)


def kernel(x, mix_norm_g, w_in, pool_w, pool_scale, q_norm_g, k_norm_g, rel_bias,
           w_out, mlp_norm_g, w_up, w_down):
    batch, seq, d = x.shape
    assert d == D_MODEL and seq % SUPER == 0 and seq % TM_PROJ == 0
    x2 = x.reshape(batch * seq, d)
    tile_heads = lambda g: jnp.tile(g.astype(F32), N_HEADS).reshape(1, ATTN_WIDTH)

    ypool, q, k, v, w_out16, w_up16, w_down16 = _proj_call(
        x2, mix_norm_g.reshape(1, d), w_in,
        tile_heads(q_norm_g) * (HEAD_DIM ** -0.5), tile_heads(k_norm_g), pool_w.astype(BF16),
        pool_scale.reshape(1, POOL_WIDTH), (w_out, w_up, w_down), seq)
    bias = _bias_call(rel_bias)
    yattn = _attn_call(q, k, v, bias, seq)
    y = _mlp_call(x2, ypool, yattn, w_out16, mlp_norm_g.reshape(1, d), w_up16, w_down16)
    return y.reshape(batch, seq, d)
```

```python
import functools
import math

import numpy as np
import jax
import jax.numpy as jnp
from jax import lax
from jax.experimental import pallas as pl
from jax.experimental.pallas import tpu as pltpu

D_MODEL = 1024
POOL_WIDTH = 512
POOL_WINDOWS = (2, 4, 8, 16)
POOL_GROUP_DIM = 128
ATTN_WIDTH = 512
HEAD_DIM = 64
N_HEADS = 8
DILATIONS = (1, 4, 16)
WINDOW_STEPS = 128
N_BUCKETS = 32
MAX_DISTANCE = 2048
D_FF = 4096
NORM_EPS = 1e-6
NEG_INF = -1e30

LANES = 128
N_PAIRS = ATTN_WIDTH // LANES
MAX_DIL = 16
SUPER = WINDOW_STEPS * MAX_DIL
HALO = 32
TM_PROJ = 1024
RELAY_PITCH = TM_PROJ // MAX_DIL + 8
TM_MLP = 1024
FF_CHUNK = 1024
VMEM_LIMIT = 56 * 1024 * 1024

F32 = jnp.float32
BF16 = jnp.bfloat16


def _rms(x, g):
    return x * lax.rsqrt(jnp.mean(x * x, axis=-1, keepdims=True) + NORM_EPS) * g


def _proj_kernel(x_ref, g_ref, w32_ref, qg_ref, kg_ref, pw_ref, ps_ref, wo32_ref, wu32_ref, wd32_ref,
                 ypool_ref, q_ref, k_ref, v_ref, wo_ref, wu_ref, wd_ref, w_ref, ubuf, sbuf, relay,
                 *, tiles_per_seq):
    i = pl.program_id(0)
    tm = x_ref.shape[0]

    @pl.when(i == 0)
    def _():
        w_ref[...] = w32_ref[...].astype(BF16)

    wo_ref[...] = wo32_ref[...].astype(BF16)
    wu_ref[...] = wu32_ref[...].astype(BF16)
    wd_ref[...] = wd32_ref[...].astype(BF16)

    seq_tile = i % tiles_per_seq

    @pl.when(seq_tile == 0)
    def _():
        ubuf[0:HALO, :] = jnp.zeros((HALO, POOL_WIDTH), F32)

    @pl.when(seq_tile != 0)
    def _():
        ubuf[0:HALO, :] = ubuf[tm:tm + HALO, :]

    a = _rms(x_ref[...], g_ref[...]).astype(BF16)

    def head_norm(t, gain):
        sq = t * t
        head0 = lax.broadcasted_iota(jnp.int32, (tm, LANES), 1) < HEAD_DIM
        zero = jnp.zeros((tm, LANES), F32)
        ssq = []
        for c in range(0, ATTN_WIDTH, LANES):
            s = sq[:, c:c + LANES]
            s0 = jnp.sum(jnp.where(head0, s, zero), axis=-1, keepdims=True)
            s1 = jnp.sum(jnp.where(head0, zero, s), axis=-1, keepdims=True)
            ssq.append(jnp.where(head0, s0, s1))
        ms = jnp.concatenate(ssq, axis=-1) * (1.0 / HEAD_DIM)
        return t * lax.rsqrt(ms + NORM_EPS) * gain

    def put_pairs(ref, which, t):
        rows = tm // MAX_DIL
        for hp in range(N_PAIRS):
            for g in range(tm // 8):
                j, r0 = g // 2, 8 * (g % 2)
                relay[which, hp, pl.ds(r0 * RELAY_PITCH + j, 8, stride=RELAY_PITCH), :] = (
                    t[8 * g:8 * g + 8, hp * LANES:(hp + 1) * LANES])
        for hp in range(N_PAIRS):
            for r in range(MAX_DIL):
                ref[hp, :, r * LANES:(r + 1) * LANES] = (
                    relay[which, hp, r * RELAY_PITCH:r * RELAY_PITCH + rows, :].astype(BF16))

    u = jnp.dot(a, w_ref[:, :POOL_WIDTH], preferred_element_type=F32)
    ubuf[HALO:HALO + tm, :] = u
    pos = seq_tile * tm + lax.broadcasted_iota(jnp.int32, (tm, 1), 0)
    pooled = []
    end = HALO + tm
    for g, w in enumerate(POOL_WINDOWS):
        cols = slice(g * POOL_GROUP_DIM, (g + 1) * POOL_GROUP_DIM)
        levels = int(math.log2(w))
        src, span = ubuf, 1
        for lvl in range(levels):
            lo = HALO if lvl == levels - 1 else HALO - 8 * (levels - 1 - lvl)
            summed = src[lo:end, cols] + src[lo - span:end - span, cols]
            span *= 2
            if lvl == levels - 1:
                wsum = summed
            else:
                sbuf[lvl, lo:end, cols] = summed
                src = sbuf.at[lvl]
        inv_count = 1.0 / jnp.minimum(pos + 1, w).astype(F32)
        pooled.append((wsum * inv_count - ubuf[HALO:end, cols]).astype(BF16))

    q = jnp.dot(a, w_ref[:, POOL_WIDTH:POOL_WIDTH + ATTN_WIDTH], preferred_element_type=F32)
    put_pairs(q_ref, 0, head_norm(q, qg_ref[...]))
    k = jnp.dot(a, w_ref[:, POOL_WIDTH + ATTN_WIDTH:POOL_WIDTH + 2 * ATTN_WIDTH],
                preferred_element_type=F32)
    put_pairs(k_ref, 1, head_norm(k, kg_ref[...]))
    v = jnp.dot(a, w_ref[:, POOL_WIDTH + 2 * ATTN_WIDTH:], preferred_element_type=F32)
    put_pairs(v_ref, 2, v)

    for g in range(len(POOL_WINDOWS)):
        cols = slice(g * POOL_GROUP_DIM, (g + 1) * POOL_GROUP_DIM)
        mixed = jnp.dot(pooled[g], pw_ref[g], preferred_element_type=F32)
        ypool_ref[:, cols] = (mixed * ps_ref[:, cols]).astype(BF16)


def _proj_call(x2, mix_g, w_in, qg, kg, pool_w, pool_scale, later_weights, seq):
    n = x2.shape[0]
    tm = TM_PROJ
    steps = n // tm
    const = lambda shape, **kw: pl.BlockSpec(shape, lambda i: (0,) * len(shape), **kw)
    row_slice = lambda wgt: pl.BlockSpec((wgt.shape[0] // steps, wgt.shape[1]), lambda i: (i, 0))
    pair_spec = pl.BlockSpec((N_PAIRS, tm // MAX_DIL, MAX_DIL * LANES), lambda i: (0, i, 0))
    pair_shape = jax.ShapeDtypeStruct((N_PAIRS, n // MAX_DIL, MAX_DIL * LANES), BF16)
    return pl.pallas_call(
        functools.partial(_proj_kernel, tiles_per_seq=seq // tm),
        grid=(n // tm,),
        in_specs=[
            pl.BlockSpec((tm, D_MODEL), lambda i: (i, 0)),
            const((1, D_MODEL)),
            const((D_MODEL, POOL_WIDTH + 3 * ATTN_WIDTH), pipeline_mode=pl.Buffered(1)),
            const((1, ATTN_WIDTH)),
            const((1, ATTN_WIDTH)),
            const((len(POOL_WINDOWS), POOL_GROUP_DIM, POOL_GROUP_DIM)),
            const((1, POOL_WIDTH)),
        ] + [row_slice(wgt) for wgt in later_weights],
        out_specs=[pl.BlockSpec((tm, POOL_WIDTH), lambda i: (i, 0)), pair_spec, pair_spec, pair_spec]
        + [row_slice(wgt) for wgt in later_weights],
        out_shape=[jax.ShapeDtypeStruct((n, POOL_WIDTH), BF16), pair_shape, pair_shape, pair_shape]
        + [jax.ShapeDtypeStruct(wgt.shape, BF16) for wgt in later_weights],
        scratch_shapes=[pltpu.VMEM((D_MODEL, POOL_WIDTH + 3 * ATTN_WIDTH), BF16),
                        pltpu.VMEM((HALO + tm, POOL_WIDTH), F32),
                        pltpu.VMEM((3, HALO + tm, POOL_WIDTH), F32),
                        pltpu.VMEM((3, N_PAIRS, MAX_DIL * RELAY_PITCH, LANES), F32)],
        compiler_params=pltpu.CompilerParams(dimension_semantics=("arbitrary",),
                                             vmem_limit_bytes=VMEM_LIMIT),
        name="proj",
    )(x2, mix_g, w_in, qg, kg, pool_w, pool_scale, *later_weights)


def _block_orders():
    out = []
    for dil in DILATIONS:
        ns = MAX_DIL // dil
        c = WINDOW_STEPS // ns
        slab = np.arange(ns)[:, None]
        q_step = (np.arange(c)[None, :] * ns + slab + WINDOW_STEPS).reshape(-1)
        k_step = np.concatenate([q_step - WINDOW_STEPS, q_step])
        out.append((dil, ns, c, q_step, k_step))
    return out


def _bucket_tables():
    max_exact = N_BUCKETS // 2
    buckets, prev_cols = [], []
    for dil, ns, c, q_step, k_step in _block_orders():
        dist = q_step[:, None] - k_step[None, :]
        ok = (dist >= 0) & (dist <= WINDOW_STEPS)
        tok = np.clip(dist, 0, WINDOW_STEPS) * dil
        d_f = np.maximum(tok, 1).astype(np.float32)
        large = max_exact + (np.log(d_f / np.float32(max_exact)) / np.float32(math.log(MAX_DISTANCE / max_exact))
                             * np.float32(N_BUCKETS - max_exact)).astype(np.int32)
        large = np.minimum(large, N_BUCKETS - 1)
        bucket = np.where(tok < max_exact, tok, large)
        buckets.append(np.where(ok, bucket, -1).astype(np.int32))
        prev_cols.append(np.broadcast_to((k_step < WINDOW_STEPS)[None, :], dist.shape).astype(np.int32))
    return np.stack(buckets), np.stack(prev_cols)


def _bias_kernel(rel_ref, bucket_ref, prev_ref, out_ref):
    bucket = bucket_ref[0]
    is_prev = prev_ref[0] != 0
    for h in range(N_HEADS):
        tab = jnp.full(bucket.shape, NEG_INF, F32)
        for b in range(N_BUCKETS):
            tab = jnp.where(bucket == b, rel_ref[b, h], tab)
        rows = slice((h % 2) * WINDOW_STEPS, (h % 2 + 1) * WINDOW_STEPS)
        out_ref[0, 0, h // 2, rows, :] = tab
        out_ref[0, 1, h // 2, rows, :] = jnp.where(is_prev, NEG_INF, tab)


def _bias_call(rel_bias):
    bucket, prev_cols = _bucket_tables()
    npat = len(DILATIONS)
    tab_spec = pl.BlockSpec((1, WINDOW_STEPS, 2 * WINDOW_STEPS), lambda p: (p, 0, 0))
    return pl.pallas_call(
        _bias_kernel,
        grid=(npat,),
        in_specs=[pl.BlockSpec(memory_space=pltpu.SMEM), tab_spec, tab_spec],
        out_specs=pl.BlockSpec((1, 2, N_PAIRS, 2 * WINDOW_STEPS, 2 * WINDOW_STEPS),
                               lambda p: (p, 0, 0, 0, 0)),
        out_shape=jax.ShapeDtypeStruct((npat, 2, N_PAIRS, 2 * WINDOW_STEPS, 2 * WINDOW_STEPS), F32),
        name="bias",
    )(rel_bias, jnp.asarray(bucket), jnp.asarray(prev_cols))


TILES = 2


def _pattern_blocks(p):
    dil = DILATIONS[p]
    ns = MAX_DIL // dil
    c = WINDOW_STEPS // ns
    return [(a * ns + n, [a + dil * j for j in range(ns)], c * n, c, a, n)
            for a in range(dil) for n in range(ns)]


def _attn_kernel(q_ref, k_ref, v_ref, bias_ref, o_ref,
                 qa32, qb32, k32, v32, kc_s, vc_s, acc_s, m_s, l_s, *, supers_per_seq):
    step = pl.program_id(1)
    is_first = (step * TILES) % supers_per_seq == 0
    first = jnp.where(is_first, 1, 0)
    w = WINDOW_STEPS
    npat = len(DILATIONS)

    @pl.when(is_first)
    def _():
        for p in range(npat):
            ns = MAX_DIL // DILATIONS[p]
            for a in range(DILATIONS[p]):
                kc_s[TILES - 1, p, a * ns + ns - 1] = jnp.zeros((w, LANES), BF16)
                vc_s[TILES - 1, p, a * ns + ns - 1] = jnp.zeros((w, LANES), BF16)

    head0 = lax.broadcasted_iota(jnp.int32, (w, LANES), 1) < HEAD_DIM

    def load_slabs(t):
        rows = slice(t * w, (t + 1) * w)
        for r in range(MAX_DIL):
            cols = slice(r * LANES, (r + 1) * LANES)
            q = q_ref[rows, cols].astype(F32)
            zero = jnp.zeros_like(q)
            qa32[t, r] = jnp.where(head0, q, zero)
            qb32[t, r] = jnp.where(head0, zero, q)
            k32[t, r] = k_ref[rows, cols].astype(F32)
            v32[t, r] = v_ref[rows, cols].astype(F32)

    def gather(ref, t, slabs, row0, c):
        return jnp.concatenate([ref[t, s, row0:row0 + c, :] for s in slabs], axis=0)

    def attend(t, p, blk):
        b, slabs, row0, c, stream, n = blk
        ns = MAX_DIL // DILATIONS[p]
        k_cur = gather(k32, t, slabs, row0, c).astype(BF16)
        v_cur = gather(v32, t, slabs, row0, c).astype(BF16)
        qm = jnp.concatenate([gather(qa32, t, slabs, row0, c), gather(qb32, t, slabs, row0, c)],
                             axis=0).astype(BF16)
        if n:
            k_prev, v_prev = kc_s[t, p, b - 1], vc_s[t, p, b - 1]
        elif t:
            k_prev, v_prev = kc_s[t - 1, p, b + ns - 1], vc_s[t - 1, p, b + ns - 1]
        else:
            k_prev, v_prev = kc_s[TILES - 1, p, b + ns - 1], vc_s[TILES - 1, p, b + ns - 1]
        kk = jnp.concatenate([k_prev, k_cur], axis=0)
        vv = jnp.concatenate([v_prev, v_cur], axis=0)
        kc_s[t, p, b] = k_cur
        vc_s[t, p, b] = v_cur
        bias = bias_ref[p, first] if (n == 0 and t == 0) else bias_ref[p, 0]
        s = lax.dot_general(qm, kk, (((1,), (1,)), ((), ())), preferred_element_type=F32) + bias
        m = jnp.max(s, axis=-1, keepdims=True)
        e = jnp.exp(s - m).astype(BF16)
        pv = jnp.dot(e, jnp.concatenate([vv, jnp.ones_like(vv)], axis=1), preferred_element_type=F32)
        for hh in range(2):
            lanes = slice(hh * HEAD_DIM, (hh + 1) * HEAD_DIM)
            m_h = jnp.broadcast_to(m[hh * w:(hh + 1) * w], (w, HEAD_DIM))
            for j, sl in enumerate(slabs):
                src = slice(hh * w + j * c, hh * w + (j + 1) * c)
                acc_s[t, p, sl, row0:row0 + c, lanes] = pv[src, lanes]
                m_s[t, p, sl, row0:row0 + c, lanes] = m_h[j * c:(j + 1) * c]
                l_s[t, p, sl, row0:row0 + c, lanes] = pv[src, LANES + hh * HEAD_DIM:LANES + (hh + 1) * HEAD_DIM]

    def combine(t, r):
        m0, m1, m2 = m_s[t, 0, r], m_s[t, 1, r], m_s[t, 2, r]
        m_all = jnp.maximum(jnp.maximum(m0, m1), m2)
        w0, w1, w2 = jnp.exp(m0 - m_all), jnp.exp(m1 - m_all), jnp.exp(m2 - m_all)
        num = w0 * acc_s[t, 0, r] + w1 * acc_s[t, 1, r] + w2 * acc_s[t, 2, r]
        den = w0 * l_s[t, 0, r] + w1 * l_s[t, 1, r] + w2 * l_s[t, 2, r]
        o_ref[t * w:(t + 1) * w, r * LANES:(r + 1) * LANES] = (num / den).astype(BF16)

    for t in range(TILES):
        load_slabs(t)
    for t in range(TILES):
        for p in reversed(range(npat)):
            for blk in _pattern_blocks(p):
                attend(t, p, blk)
        for r in range(MAX_DIL):
            combine(t, r)


def _attn_call(q, k, v, bias, seq):
    rows, width = q.shape[1:]
    supers_per_seq = seq // SUPER
    assert supers_per_seq % TILES == 0
    w = WINDOW_STEPS
    tok = pl.BlockSpec((None, TILES * w, width), lambda hp, st: (hp, st, 0))
    npat = len(DILATIONS)
    slab = pltpu.VMEM((TILES, MAX_DIL, w, LANES), F32)
    part = pltpu.VMEM((TILES, npat, MAX_DIL, w, LANES), F32)
    chunks = pltpu.VMEM((TILES, npat, MAX_DIL, w, LANES), BF16)
    return pl.pallas_call(
        functools.partial(_attn_kernel, supers_per_seq=supers_per_seq),
        grid=(N_PAIRS, rows // (TILES * w)),
        in_specs=[tok, tok, tok,
                  pl.BlockSpec((npat, 2, None, 2 * w, 2 * w), lambda hp, st: (0, 0, hp, 0, 0))],
        out_specs=tok,
        out_shape=jax.ShapeDtypeStruct((N_PAIRS, rows, width), BF16),
        scratch_shapes=[slab, slab, slab, slab, chunks, chunks, part, part, part],
        compiler_params=pltpu.CompilerParams(dimension_semantics=("arbitrary", "arbitrary"),
                                             vmem_limit_bytes=VMEM_LIMIT),
        name="attn",
    )(q, k, v, bias)


def _mlp_kernel(x_ref, ypool_ref, yattn_ref, wo_ref, g_ref, wu_ref, wd_ref, o_ref, relay):
    tm = x_ref.shape[0]
    for hp in range(N_PAIRS):
        for r in range(MAX_DIL):
            relay[hp, pl.ds(r, tm // MAX_DIL, stride=MAX_DIL), :] = (
                yattn_ref[hp, :, r * LANES:(r + 1) * LANES].astype(F32))
    mixed = jnp.concatenate([ypool_ref[...]] + [relay[hp].astype(BF16) for hp in range(N_PAIRS)], axis=-1)
    h = x_ref[...] + jnp.dot(mixed, wo_ref[...], preferred_element_type=F32)
    c = _rms(h, g_ref[...]).astype(BF16)
    acc = h
    for f in range(0, D_FF, FF_CHUNK):
        up = jnp.dot(c, wu_ref[:, f:f + FF_CHUNK], preferred_element_type=F32)
        ff = jnp.square(jnp.maximum(up, 0.0)).astype(BF16)
        acc = acc + jnp.dot(ff, wd_ref[f:f + FF_CHUNK, :], preferred_element_type=F32)
    o_ref[...] = acc


def _mlp_call(x2, ypool, yattn, w_out, mlp_g, w_up, w_down):
    n = x2.shape[0]
    tm = TM_MLP
    const = lambda shape: pl.BlockSpec(shape, lambda i: (0,) * len(shape),
                                       pipeline_mode=pl.Buffered(1))
    return pl.pallas_call(
        _mlp_kernel,
        grid=(n // tm,),
        in_specs=[
            pl.BlockSpec((tm, D_MODEL), lambda i: (i, 0)),
            pl.BlockSpec((tm, POOL_WIDTH), lambda i: (i, 0)),
            pl.BlockSpec((N_PAIRS, tm // MAX_DIL, MAX_DIL * LANES), lambda i: (0, i, 0)),
            const((D_MODEL, D_MODEL)),
            const((1, D_MODEL)),
            const((D_MODEL, D_FF)),
            const((D_FF, D_MODEL)),
        ],
        out_specs=pl.BlockSpec((tm, D_MODEL), lambda i: (i, 0)),
        out_shape=jax.ShapeDtypeStruct((n, D_MODEL), F32),
        scratch_shapes=[pltpu.VMEM((N_PAIRS, tm, LANES), F32)],
        compiler_params=pltpu.CompilerParams(dimension_semantics=("arbitrary",),
                                             vmem_limit_bytes=VMEM_LIMIT),
        name="mlp",
    )(x2, ypool, yattn, w_out, mlp_g, w_up, w_down)


def kernel(x, mix_norm_g, w_in, pool_w, pool_scale, q_norm_g, k_norm_g, rel_bias,
           w_out, mlp_norm_g, w_up, w_down):
    batch, seq, d = x.shape
    assert d == D_MODEL and seq % SUPER == 0 and seq % TM_PROJ == 0
    x2 = x.reshape(batch * seq, d)
    tile_heads = lambda g: jnp.tile(g.astype(F32), N_HEADS).reshape(1, ATTN_WIDTH)

    ypool, q, k, v, w_out16, w_up16, w_down16 = _proj_call(
        x2, mix_norm_g.reshape(1, d), w_in,
        tile_heads(q_norm_g) * (HEAD_DIM ** -0.5), tile_heads(k_norm_g), pool_w.astype(BF16),
        pool_scale.reshape(1, POOL_WIDTH), (w_out, w_up, w_down), seq)
    bias = _bias_call(rel_bias)
    yattn = _attn_call(q, k, v, bias, seq)
    y = _mlp_call(x2, ypool, yattn, w_out16, mlp_norm_g.reshape(1, d), w_up16, w_down16)
    return y.reshape(batch, seq, d)
```

```python
import functools
import math

import numpy as np
import jax
import jax.numpy as jnp
from jax import lax
from jax.experimental import pallas as pl
from jax.experimental.pallas import tpu as pltpu

D_MODEL = 1024
POOL_WIDTH = 512
POOL_WINDOWS = (2, 4, 8, 16)
POOL_GROUP_DIM = 128
ATTN_WIDTH = 512
HEAD_DIM = 64
N_HEADS = 8
DILATIONS = (1, 4, 16)
WINDOW_STEPS = 128
N_BUCKETS = 32
MAX_DISTANCE = 2048
D_FF = 4096
NORM_EPS = 1e-6
NEG_INF = -1e30

LANES = 128
N_PAIRS = ATTN_WIDTH // LANES
MAX_DIL = 16
SUPER = WINDOW_STEPS * MAX_DIL
HALO = 32
TM_PROJ = 1024
RELAY_PITCH = TM_PROJ // MAX_DIL + 8
TM_MLP = 1024
FF_CHUNK = 2048
VMEM_LIMIT = 56 * 1024 * 1024

F32 = jnp.float32
BF16 = jnp.bfloat16


def _rms(x, g):
    return x * lax.rsqrt(jnp.mean(x * x, axis=-1, keepdims=True) + NORM_EPS) * g


def _proj_kernel(x_ref, g_ref, w32_ref, qg_ref, kg_ref, pw_ref, ps_ref, wo32_ref, wu32_ref, wd32_ref,
                 ypool_ref, q_ref, k_ref, v_ref, wo_ref, wu_ref, wd_ref, w_ref, ubuf, sbuf, relay,
                 *, tiles_per_seq):
    i = pl.program_id(0)
    tm = x_ref.shape[0]

    @pl.when(i == 0)
    def _():
        w_ref[...] = w32_ref[...].astype(BF16)

    wo_ref[...] = wo32_ref[...].astype(BF16)
    wu_ref[...] = wu32_ref[...].astype(BF16)
    wd_ref[...] = wd32_ref[...].astype(BF16)

    seq_tile = i % tiles_per_seq

    @pl.when(seq_tile == 0)
    def _():
        ubuf[0:HALO, :] = jnp.zeros((HALO, POOL_WIDTH), F32)

    @pl.when(seq_tile != 0)
    def _():
        ubuf[0:HALO, :] = ubuf[tm:tm + HALO, :]

    a = _rms(x_ref[...], g_ref[...]).astype(BF16)

    def head_norm(t, gain):
        sq = t * t
        head0 = lax.broadcasted_iota(jnp.int32, (tm, LANES), 1) < HEAD_DIM
        zero = jnp.zeros((tm, LANES), F32)
        ssq = []
        for c in range(0, ATTN_WIDTH, LANES):
            s = sq[:, c:c + LANES]
            s0 = jnp.sum(jnp.where(head0, s, zero), axis=-1, keepdims=True)
            s1 = jnp.sum(jnp.where(head0, zero, s), axis=-1, keepdims=True)
            ssq.append(jnp.where(head0, s0, s1))
        ms = jnp.concatenate(ssq, axis=-1) * (1.0 / HEAD_DIM)
        return t * lax.rsqrt(ms + NORM_EPS) * gain

    def put_pairs(ref, which, t):
        rows = tm // MAX_DIL
        for hp in range(N_PAIRS):
            for g in range(tm // 8):
                j, r0 = g // 2, 8 * (g % 2)
                relay[which, hp, pl.ds(r0 * RELAY_PITCH + j, 8, stride=RELAY_PITCH), :] = (
                    t[8 * g:8 * g + 8, hp * LANES:(hp + 1) * LANES])
        for hp in range(N_PAIRS):
            for r in range(MAX_DIL):
                ref[hp, :, r * LANES:(r + 1) * LANES] = (
                    relay[which, hp, r * RELAY_PITCH:r * RELAY_PITCH + rows, :].astype(BF16))

    u = jnp.dot(a, w_ref[:, :POOL_WIDTH], preferred_element_type=F32)
    ubuf[HALO:HALO + tm, :] = u
    pos = seq_tile * tm + lax.broadcasted_iota(jnp.int32, (tm, 1), 0)
    pooled = []
    end = HALO + tm
    for g, w in enumerate(POOL_WINDOWS):
        cols = slice(g * POOL_GROUP_DIM, (g + 1) * POOL_GROUP_DIM)
        levels = int(math.log2(w))
        src, span = ubuf, 1
        for lvl in range(levels):
            lo = HALO if lvl == levels - 1 else HALO - 8 * (levels - 1 - lvl)
            summed = src[lo:end, cols] + src[lo - span:end - span, cols]
            span *= 2
            if lvl == levels - 1:
                wsum = summed
            else:
                sbuf[lvl, lo:end, cols] = summed
                src = sbuf.at[lvl]
        inv_count = 1.0 / jnp.minimum(pos + 1, w).astype(F32)
        pooled.append((wsum * inv_count - ubuf[HALO:end, cols]).astype(BF16))

    q = jnp.dot(a, w_ref[:, POOL_WIDTH:POOL_WIDTH + ATTN_WIDTH], preferred_element_type=F32)
    put_pairs(q_ref, 0, head_norm(q, qg_ref[...]))
    k = jnp.dot(a, w_ref[:, POOL_WIDTH + ATTN_WIDTH:POOL_WIDTH + 2 * ATTN_WIDTH],
                preferred_element_type=F32)
    put_pairs(k_ref, 1, head_norm(k, kg_ref[...]))
    v = jnp.dot(a, w_ref[:, POOL_WIDTH + 2 * ATTN_WIDTH:], preferred_element_type=F32)
    put_pairs(v_ref, 2, v)

    for g in range(len(POOL_WINDOWS)):
        cols = slice(g * POOL_GROUP_DIM, (g + 1) * POOL_GROUP_DIM)
        mixed = jnp.dot(pooled[g], pw_ref[g], preferred_element_type=F32)
        ypool_ref[:, cols] = (mixed * ps_ref[:, cols]).astype(BF16)


def _proj_call(x2, mix_g, w_in, qg, kg, pool_w, pool_scale, later_weights, seq):
    n = x2.shape[0]
    tm = TM_PROJ
    steps = n // tm
    const = lambda shape, **kw: pl.BlockSpec(shape, lambda i: (0,) * len(shape), **kw)
    row_slice = lambda wgt: pl.BlockSpec((wgt.shape[0] // steps, wgt.shape[1]), lambda i: (i, 0))
    pair_spec = pl.BlockSpec((N_PAIRS, tm // MAX_DIL, MAX_DIL * LANES), lambda i: (0, i, 0))
    pair_shape = jax.ShapeDtypeStruct((N_PAIRS, n // MAX_DIL, MAX_DIL * LANES), BF16)
    return pl.pallas_call(
        functools.partial(_proj_kernel, tiles_per_seq=seq // tm),
        grid=(n // tm,),
        in_specs=[
            pl.BlockSpec((tm, D_MODEL), lambda i: (i, 0)),
            const((1, D_MODEL)),
            const((D_MODEL, POOL_WIDTH + 3 * ATTN_WIDTH), pipeline_mode=pl.Buffered(1)),
            const((1, ATTN_WIDTH)),
            const((1, ATTN_WIDTH)),
            const((len(POOL_WINDOWS), POOL_GROUP_DIM, POOL_GROUP_DIM)),
            const((1, POOL_WIDTH)),
        ] + [row_slice(wgt) for wgt in later_weights],
        out_specs=[pl.BlockSpec((tm, POOL_WIDTH), lambda i: (i, 0)), pair_spec, pair_spec, pair_spec]
        + [row_slice(wgt) for wgt in later_weights],
        out_shape=[jax.ShapeDtypeStruct((n, POOL_WIDTH), BF16), pair_shape, pair_shape, pair_shape]
        + [jax.ShapeDtypeStruct(wgt.shape, BF16) for wgt in later_weights],
        scratch_shapes=[pltpu.VMEM((D_MODEL, POOL_WIDTH + 3 * ATTN_WIDTH), BF16),
                        pltpu.VMEM((HALO + tm, POOL_WIDTH), F32),
                        pltpu.VMEM((3, HALO + tm, POOL_WIDTH), F32),
                        pltpu.VMEM((3, N_PAIRS, MAX_DIL * RELAY_PITCH, LANES), F32)],
        compiler_params=pltpu.CompilerParams(dimension_semantics=("arbitrary",),
                                             vmem_limit_bytes=VMEM_LIMIT),
        name="proj",
    )(x2, mix_g, w_in, qg, kg, pool_w, pool_scale, *later_weights)


def _block_orders():
    out = []
    for dil in DILATIONS:
        ns = MAX_DIL // dil
        c = WINDOW_STEPS // ns
        slab = np.arange(ns)[:, None]
        q_step = (np.arange(c)[None, :] * ns + slab + WINDOW_STEPS).reshape(-1)
        k_step = np.concatenate([q_step - WINDOW_STEPS, q_step])
        out.append((dil, ns, c, q_step, k_step))
    return out


def _bucket_tables():
    max_exact = N_BUCKETS // 2
    buckets, prev_cols = [], []
    for dil, ns, c, q_step, k_step in _block_orders():
        dist = q_step[:, None] - k_step[None, :]
        ok = (dist >= 0) & (dist <= WINDOW_STEPS)
        tok = np.clip(dist, 0, WINDOW_STEPS) * dil
        d_f = np.maximum(tok, 1).astype(np.float32)
        large = max_exact + (np.log(d_f / np.float32(max_exact)) / np.float32(math.log(MAX_DISTANCE / max_exact))
                             * np.float32(N_BUCKETS - max_exact)).astype(np.int32)
        large = np.minimum(large, N_BUCKETS - 1)
        bucket = np.where(tok < max_exact, tok, large)
        buckets.append(np.where(ok, bucket, -1).astype(np.int32))
        prev_cols.append(np.broadcast_to((k_step < WINDOW_STEPS)[None, :], dist.shape).astype(np.int32))
    return np.stack(buckets), np.stack(prev_cols)


def _bias_kernel(rel_ref, bucket_ref, prev_ref, out_ref):
    bucket = bucket_ref[0]
    is_prev = prev_ref[0] != 0
    for h in range(N_HEADS):
        tab = jnp.full(bucket.shape, NEG_INF, F32)
        for b in range(N_BUCKETS):
            tab = jnp.where(bucket == b, rel_ref[b, h], tab)
        rows = slice((h % 2) * WINDOW_STEPS, (h % 2 + 1) * WINDOW_STEPS)
        out_ref[0, 0, h // 2, rows, :] = tab
        out_ref[0, 1, h // 2, rows, :] = jnp.where(is_prev, NEG_INF, tab)


def _bias_call(rel_bias):
    bucket, prev_cols = _bucket_tables()
    npat = len(DILATIONS)
    tab_spec = pl.BlockSpec((1, WINDOW_STEPS, 2 * WINDOW_STEPS), lambda p: (p, 0, 0))
    return pl.pallas_call(
        _bias_kernel,
        grid=(npat,),
        in_specs=[pl.BlockSpec(memory_space=pltpu.SMEM), tab_spec, tab_spec],
        out_specs=pl.BlockSpec((1, 2, N_PAIRS, 2 * WINDOW_STEPS, 2 * WINDOW_STEPS),
                               lambda p: (p, 0, 0, 0, 0)),
        out_shape=jax.ShapeDtypeStruct((npat, 2, N_PAIRS, 2 * WINDOW_STEPS, 2 * WINDOW_STEPS), F32),
        name="bias",
    )(rel_bias, jnp.asarray(bucket), jnp.asarray(prev_cols))


TILES = 2


def _pattern_blocks(p):
    dil = DILATIONS[p]
    ns = MAX_DIL // dil
    c = WINDOW_STEPS // ns
    return [(a * ns + n, [a + dil * j for j in range(ns)], c * n, c, a, n)
            for a in range(dil) for n in range(ns)]


def _attn_kernel(q_ref, k_ref, v_ref, bias_ref, o_ref,
                 qa32, qb32, k32, v32, kc_s, vc_s, acc_s, m_s, l_s, *, supers_per_seq):
    step = pl.program_id(1)
    is_first = (step * TILES) % supers_per_seq == 0
    first = jnp.where(is_first, 1, 0)
    w = WINDOW_STEPS
    npat = len(DILATIONS)

    @pl.when(is_first)
    def _():
        for p in range(npat):
            ns = MAX_DIL // DILATIONS[p]
            for a in range(DILATIONS[p]):
                kc_s[TILES - 1, p, a * ns + ns - 1] = jnp.zeros((w, LANES), BF16)
                vc_s[TILES - 1, p, a * ns + ns - 1] = jnp.zeros((w, LANES), BF16)

    head0 = lax.broadcasted_iota(jnp.int32, (w, LANES), 1) < HEAD_DIM

    def load_slabs(t):
        rows = slice(t * w, (t + 1) * w)
        for r in range(MAX_DIL):
            cols = slice(r * LANES, (r + 1) * LANES)
            q = q_ref[rows, cols].astype(F32)
            zero = jnp.zeros_like(q)
            qa32[t, r] = jnp.where(head0, q, zero)
            qb32[t, r] = jnp.where(head0, zero, q)
            k32[t, r] = k_ref[rows, cols].astype(F32)
            v32[t, r] = v_ref[rows, cols].astype(F32)

    def gather(ref, t, slabs, row0, c):
        return jnp.concatenate([ref[t, s, row0:row0 + c, :] for s in slabs], axis=0)

    def attend(t, p, blk):
        b, slabs, row0, c, stream, n = blk
        ns = MAX_DIL // DILATIONS[p]
        k_cur = gather(k32, t, slabs, row0, c).astype(BF16)
        v_cur = gather(v32, t, slabs, row0, c).astype(BF16)
        qm = jnp.concatenate([gather(qa32, t, slabs, row0, c), gather(qb32, t, slabs, row0, c)],
                             axis=0).astype(BF16)
        if n:
            k_prev, v_prev = kc_s[t, p, b - 1], vc_s[t, p, b - 1]
        elif t:
            k_prev, v_prev = kc_s[t - 1, p, b + ns - 1], vc_s[t - 1, p, b + ns - 1]
        else:
            k_prev, v_prev = kc_s[TILES - 1, p, b + ns - 1], vc_s[TILES - 1, p, b + ns - 1]
        kk = jnp.concatenate([k_prev, k_cur], axis=0)
        vv = jnp.concatenate([v_prev, v_cur], axis=0)
        kc_s[t, p, b] = k_cur
        vc_s[t, p, b] = v_cur
        bias = bias_ref[p, first] if (n == 0 and t == 0) else bias_ref[p, 0]
        s = lax.dot_general(qm, kk, (((1,), (1,)), ((), ())), preferred_element_type=F32) + bias
        m = jnp.max(s, axis=-1, keepdims=True)
        e = jnp.exp(s - m).astype(BF16)
        pv = jnp.dot(e, jnp.concatenate([vv, jnp.ones_like(vv)], axis=1), preferred_element_type=F32)
        for hh in range(2):
            lanes = slice(hh * HEAD_DIM, (hh + 1) * HEAD_DIM)
            m_h = jnp.broadcast_to(m[hh * w:(hh + 1) * w], (w, HEAD_DIM))
            for j, sl in enumerate(slabs):
                src = slice(hh * w + j * c, hh * w + (j + 1) * c)
                acc_s[t, p, sl, row0:row0 + c, lanes] = pv[src, lanes]
                m_s[t, p, sl, row0:row0 + c, lanes] = m_h[j * c:(j + 1) * c]
                l_s[t, p, sl, row0:row0 + c, lanes] = pv[src, LANES + hh * HEAD_DIM:LANES + (hh + 1) * HEAD_DIM]

    def combine(t, r):
        m0, m1, m2 = m_s[t, 0, r], m_s[t, 1, r], m_s[t, 2, r]
        m_all = jnp.maximum(jnp.maximum(m0, m1), m2)
        w0, w1, w2 = jnp.exp(m0 - m_all), jnp.exp(m1 - m_all), jnp.exp(m2 - m_all)
        num = w0 * acc_s[t, 0, r] + w1 * acc_s[t, 1, r] + w2 * acc_s[t, 2, r]
        den = w0 * l_s[t, 0, r] + w1 * l_s[t, 1, r] + w2 * l_s[t, 2, r]
        o_ref[t * w:(t + 1) * w, r * LANES:(r + 1) * LANES] = (num / den).astype(BF16)

    for t in range(TILES):
        load_slabs(t)
    for t in range(TILES):
        for p in reversed(range(npat)):
            for blk in _pattern_blocks(p):
                attend(t, p, blk)
        for r in range(MAX_DIL):
            combine(t, r)


def _attn_call(q, k, v, bias, seq):
    rows, width = q.shape[1:]
    supers_per_seq = seq // SUPER
    assert supers_per_seq % TILES == 0
    w = WINDOW_STEPS
    tok = pl.BlockSpec((None, TILES * w, width), lambda hp, st: (hp, st, 0))
    npat = len(DILATIONS)
    slab = pltpu.VMEM((TILES, MAX_DIL, w, LANES), F32)
    part = pltpu.VMEM((TILES, npat, MAX_DIL, w, LANES), F32)
    chunks = pltpu.VMEM((TILES, npat, MAX_DIL, w, LANES), BF16)
    return pl.pallas_call(
        functools.partial(_attn_kernel, supers_per_seq=supers_per_seq),
        grid=(N_PAIRS, rows // (TILES * w)),
        in_specs=[tok, tok, tok,
                  pl.BlockSpec((npat, 2, None, 2 * w, 2 * w), lambda hp, st: (0, 0, hp, 0, 0))],
        out_specs=tok,
        out_shape=jax.ShapeDtypeStruct((N_PAIRS, rows, width), BF16),
        scratch_shapes=[slab, slab, slab, slab, chunks, chunks, part, part, part],
        compiler_params=pltpu.CompilerParams(dimension_semantics=("arbitrary", "arbitrary"),
                                             vmem_limit_bytes=VMEM_LIMIT),
        name="attn",
    )(q, k, v, bias)


def _mlp_kernel(x_ref, ypool_ref, yattn_ref, wo_ref, g_ref, wu_ref, wd_ref, o_ref, relay):
    tm = x_ref.shape[0]
    for hp in range(N_PAIRS):
        for r in range(MAX_DIL):
            relay[hp, pl.ds(r, tm // MAX_DIL, stride=MAX_DIL), :] = (
                yattn_ref[hp, :, r * LANES:(r + 1) * LANES].astype(F32))
    mixed = jnp.concatenate([ypool_ref[...]] + [relay[hp].astype(BF16) for hp in range(N_PAIRS)], axis=-1)
    h = x_ref[...] + jnp.dot(mixed, wo_ref[...], preferred_element_type=F32)
    c = _rms(h, g_ref[...]).astype(BF16)
    acc = h
    for f in range(0, D_FF, FF_CHUNK):
        up = jnp.dot(c, wu_ref[:, f:f + FF_CHUNK], preferred_element_type=F32)
        ff = jnp.square(jnp.maximum(up, 0.0)).astype(BF16)
        acc = acc + jnp.dot(ff, wd_ref[f:f + FF_CHUNK, :], preferred_element_type=F32)
    o_ref[...] = acc


def _mlp_call(x2, ypool, yattn, w_out, mlp_g, w_up, w_down):
    n = x2.shape[0]
    tm = TM_MLP
    const = lambda shape: pl.BlockSpec(shape, lambda i: (0,) * len(shape),
                                       pipeline_mode=pl.Buffered(1))
    return pl.pallas_call(
        _mlp_kernel,
        grid=(n // tm,),
        in_specs=[
            pl.BlockSpec((tm, D_MODEL), lambda i: (i, 0)),
            pl.BlockSpec((tm, POOL_WIDTH), lambda i: (i, 0)),
            pl.BlockSpec((N_PAIRS, tm // MAX_DIL, MAX_DIL * LANES), lambda i: (0, i, 0)),
            const((D_MODEL, D_MODEL)),
            const((1, D_MODEL)),
            const((D_MODEL, D_FF)),
            const((D_FF, D_MODEL)),
        ],
        out_specs=pl.BlockSpec((tm, D_MODEL), lambda i: (i, 0)),
        out_shape=jax.ShapeDtypeStruct((n, D_MODEL), F32),
        scratch_shapes=[pltpu.VMEM((N_PAIRS, tm, LANES), F32)],
        compiler_params=pltpu.CompilerParams(dimension_semantics=("arbitrary",),
                                             vmem_limit_bytes=VMEM_LIMIT),
        name="mlp",
    )(x2, ypool, yattn, w_out, mlp_g, w_up, w_down)


def kernel(x, mix_norm_g, w_in, pool_w, pool_scale, q_norm_g, k_norm_g, rel_bias,
           w_out, mlp_norm_g, w_up, w_down):
    batch, seq, d = x.shape
    assert d == D_MODEL and seq % SUPER == 0 and seq % TM_PROJ == 0
    x2 = x.reshape(batch * seq, d)
    tile_heads = lambda g: jnp.tile(g.astype(F32), N_HEADS).reshape(1, ATTN_WIDTH)

    ypool, q, k, v, w_out16, w_up16, w_down16 = _proj_call(
        x2, mix_norm_g.reshape(1, d), w_in,
        tile_heads(q_norm_g) * (HEAD_DIM ** -0.5), tile_heads(k_norm_g), pool_w.astype(BF16),
        pool_scale.reshape(1, POOL_WIDTH), (w_out, w_up, w_down), seq)
    bias = _bias_call(rel_bias)
    yattn = _attn_call(q, k, v, bias, seq)
    y = _mlp_call(x2, ypool, yattn, w_out16, mlp_norm_g.reshape(1, d), w_up16, w_down16)
    return y.reshape(batch, seq, d)
```
